```python
import jax, jax.numpy as jnp
from jax import lax
import numpy as np

D_MODEL = 1024
BATCH = 8
SEQ = 2048
DEPTH = 2
DEC_BATCH = 128
DEC_SEQ = 8
PAST_LEN = 16384
PAGE_SIZE = 128

H_RET = 4
DK_RET = D_MODEL // 8
DV_RET = D_MODEL // 4
D_RET_QK = H_RET * DK_RET
D_RET_V = H_RET * DV_RET
RET_CHUNK = 128
ROPE_BASE = 10000.0
D_LRU = D_MODEL
N_LRU_BLOCKS = 8
LRU_BLOCK = D_LRU // N_LRU_BLOCKS
CONV_W = 4
RG_C = 8.0
D_FF = 4 * D_MODEL
NORM_EPS = 1e-6
GN_EPS = 1e-5
D_IN = 2 * D_RET_QK + 2 * D_RET_V + 2 * D_LRU + 2 * D_MODEL

kernel_name = 'hybrid_retention_rglru_adaln_step'


def rmsnorm(x, g):
    xf = x.astype(jnp.float32)
    y = xf * lax.rsqrt(jnp.mean(xf * xf, axis=-1, keepdims=True) + NORM_EPS)
    return (y * g.astype(jnp.float32)).astype(x.dtype)


def rope(x, pos):
    half = x.shape[-1] // 2
    inv = ROPE_BASE ** (-jnp.arange(half, dtype=jnp.float32) / half)
    ang = pos.astype(jnp.float32)[:, None] * inv[None, :]
    cos = jnp.cos(ang)[None, :, None, :]
    sin = jnp.sin(ang)[None, :, None, :]
    x1, x2 = x[..., :half], x[..., half:]
    return jnp.concatenate([x1 * cos - x2 * sin, x2 * cos + x1 * sin], axis=-1)


def retention(q, k, v, s0):
    B, L = q.shape[0], q.shape[1]
    C = RET_CHUNK if L % RET_CHUNK == 0 else L
    n = L // C
    log_g = jnp.log1p(-jnp.exp2(-5.0 - jnp.arange(H_RET, dtype=jnp.float32)))
    idx = jnp.arange(C, dtype=jnp.float32)
    diff = idx[:, None] - idx[None, :]
    intra = jnp.where(diff[None] >= 0, jnp.exp(jnp.maximum(diff, 0.0)[None] * log_g[:, None, None]), 0.0)
    q_decay = jnp.exp((idx + 1.0)[None, :] * log_g[:, None])
    k_decay = jnp.exp((C - 1.0 - idx)[None, :] * log_g[:, None])
    chunk_decay = jnp.exp(C * log_g)

    def to_chunks(t):
        return t.reshape(B, n, C, H_RET, t.shape[-1]).transpose(1, 0, 3, 2, 4)

    def step(s, qkv):
        qc, kc, vc = qkv
        scores = jnp.einsum('bhqd,bhkd->bhqk', qc, kc) * intra
        o = (jnp.einsum('bhqk,bhkv->bhqv', scores, vc)
             + jnp.einsum('bhqd,bhdv->bhqv', qc, s) * q_decay[None, :, :, None])
        s = (s * chunk_decay[None, :, None, None]
             + jnp.einsum('bhkd,bhkv->bhdv', kc * k_decay[None, :, :, None], vc))
        return s, o

    s_new, o = lax.scan(step, s0, (to_chunks(q), to_chunks(k), to_chunks(v)))
    o = o.transpose(1, 0, 3, 2, 4).reshape(B, L, H_RET, DV_RET)
    return o, s_new


def rg_lru_block(xl, gl, h0, conv_buf, conv_w, conv_b, w_rg_a, b_rg_a, w_rg_x, b_rg_x, lru_lambda):
    B, L, _ = xl.shape
    f32 = jnp.float32
    xpad = jnp.concatenate([conv_buf.astype(xl.dtype), xl], axis=1)
    conv_new = xpad[:, -(CONV_W - 1):]
    xc = lax.conv_general_dilated(xpad, conv_w[:, None, :].astype(xpad.dtype), window_strides=(1,),
                                  padding='VALID', dimension_numbers=('NWC', 'WIO', 'NWC'),
                                  feature_group_count=D_LRU) + conv_b
    xc = xc.astype(f32)
    xblk = xc.reshape(B, L, N_LRU_BLOCKS, LRU_BLOCK)
    r = jax.nn.sigmoid(jnp.einsum('blni,nij->blnj', xblk, w_rg_a.astype(f32)).reshape(B, L, D_LRU) + b_rg_a.astype(f32))
    i = jax.nn.sigmoid(jnp.einsum('blni,nij->blnj', xblk, w_rg_x.astype(f32)).reshape(B, L, D_LRU) + b_rg_x.astype(f32))
    log_a = -RG_C * r * jax.nn.softplus(-lru_lambda.astype(f32))
    a = jnp.exp(log_a)
    u = jnp.sqrt(-jnp.expm1(2.0 * log_a)) * (i * xc)

    def combine(left, right):
        a1, b1 = left
        a2, b2 = right
        return a1 * a2, a2 * b1 + b2

    a_cum, b_cum = lax.associative_scan(combine, (a, u), axis=1)
    hseq = a_cum * h0.astype(f32)[:, None, :] + b_cum
    y = hseq * jax.nn.gelu(gl.astype(f32))
    return y.astype(xl.dtype), hseq[:, -1], conv_new


def mixer(h, pos, s_ret, s_lru, s_conv, w_in, w_proj_ret, w_proj_lru, w_out,
          conv_w, conv_b, w_rg_a, b_rg_a, w_rg_x, b_rg_x, lru_lambda):
    B, L, _ = h.shape
    f32 = jnp.float32
    z = h @ w_in
    sizes = (D_RET_QK, D_RET_QK, D_RET_V, D_RET_V, D_LRU, D_LRU, D_MODEL, D_MODEL)
    offs = np.cumsum(sizes)[:-1].tolist()
    q, k, v, g, xl, gl, ma, mb = jnp.split(z, offs, axis=-1)
    q = rope(q.reshape(B, L, H_RET, DK_RET).astype(f32), pos)
    k = rope(k.reshape(B, L, H_RET, DK_RET).astype(f32), pos) * (DK_RET ** -0.5)
    v = v.reshape(B, L, H_RET, DV_RET).astype(f32)
    o, s_ret_new = retention(q, k, v, s_ret.astype(f32))
    mu = jnp.mean(o, axis=-1, keepdims=True)
    var = jnp.mean(jnp.square(o - mu), axis=-1, keepdims=True)
    o = (o - mu) * lax.rsqrt(var + GN_EPS)
    o = (o.reshape(B, L, D_RET_V) * jax.nn.silu(g.astype(f32))).astype(h.dtype)
    p_ret = o @ w_proj_ret
    y_lru, s_lru_new, s_conv_new = rg_lru_block(xl, gl, s_lru, s_conv, conv_w, conv_b,
                                                w_rg_a, b_rg_a, w_rg_x, b_rg_x, lru_lambda)
    p_lru = y_lru @ w_proj_lru
    m = jax.nn.sigmoid(ma) * p_ret + jax.nn.sigmoid(mb) * p_lru
    return m @ w_out, s_ret_new, s_lru_new, s_conv_new


def trunk(x, c, pos, s_ret, s_lru, s_conv, w_ada, b_ada, norm1_g, norm2_g, w_in, w_proj_ret,
          w_proj_lru, w_out, conv_w, conv_b, w_rg_a, b_rg_a, w_rg_x, b_rg_x, lru_lambda,
          w_mlp1, w_mlp2, final_g):
    new_ret, new_lru, new_conv = [], [], []
    for l in range(DEPTH):
        ada = jax.nn.silu(c) @ w_ada[l] + b_ada[l]
        sh1, sc1, gt1, sh2, sc2, gt2 = jnp.split(ada[:, None, :], 6, axis=-1)
        h = rmsnorm(x, norm1_g[l]) * (1.0 + sc1) + sh1
        mix, sr, sl, sc = mixer(h, pos, s_ret[l], s_lru[l], s_conv[l], w_in[l], w_proj_ret[l],
                                w_proj_lru[l], w_out[l], conv_w[l], conv_b[l], w_rg_a[l], b_rg_a[l],
                                w_rg_x[l], b_rg_x[l], lru_lambda[l])
        x = x + gt1 * mix
        h2 = rmsnorm(x, norm2_g[l]) * (1.0 + sc2) + sh2
        x = x + gt2 * (jnp.square(jax.nn.relu(h2 @ w_mlp1[l])) @ w_mlp2[l])
        new_ret.append(sr)
        new_lru.append(sl)
        new_conv.append(sc)
    y = rmsnorm(x, final_g)
    return y, jnp.stack(new_ret), jnp.stack(new_lru), jnp.stack(new_conv)


def setup_inputs(seed: int = 0) -> dict:
    key = jax.random.key(seed)
    ks = jax.random.split(key, 32)
    f32 = jnp.float32

    def nrm(k, shape, scale):
        return jax.random.normal(k, shape, f32) * scale

    u = jax.random.uniform(ks[20], (DEPTH, D_LRU), f32, 0.9, 0.999)
    a0 = u ** (1.0 / RG_C)
    lru_lambda = jnp.log(a0) - jnp.log1p(-a0)
    return {
        'x_prompt': nrm(ks[0], (BATCH, SEQ, D_MODEL), 1.0),
        'x_sample': nrm(ks[1], (DEC_BATCH, DEC_SEQ, D_MODEL), 1.0),
        'c_prompt': nrm(ks[2], (BATCH, D_MODEL), 1.0),
        'c_sample': nrm(ks[3], (DEC_BATCH, D_MODEL), 1.0),
        'state_ret': nrm(ks[4], (DEPTH, DEC_BATCH, H_RET, DK_RET, DV_RET), 0.5),
        'state_lru': nrm(ks[5], (DEPTH, DEC_BATCH, D_LRU), 0.5),
        'state_conv': nrm(ks[6], (DEPTH, DEC_BATCH, CONV_W - 1, D_LRU), 1.0),
        'w_ada': nrm(ks[7], (DEPTH, D_MODEL, 6 * D_MODEL), 0.5 * D_MODEL ** -0.5),
        'b_ada': nrm(ks[8], (DEPTH, 6 * D_MODEL), 0.01),
        'norm1_g': 1.0 + nrm(ks[9], (DEPTH, D_MODEL), 0.02),
        'norm2_g': 1.0 + nrm(ks[10], (DEPTH, D_MODEL), 0.02),
        'w_in': nrm(ks[11], (DEPTH, D_MODEL, D_IN), D_MODEL ** -0.5),
        'w_proj_ret': nrm(ks[12], (DEPTH, D_RET_V, D_MODEL), D_RET_V ** -0.5),
        'w_proj_lru': nrm(ks[13], (DEPTH, D_LRU, D_MODEL), D_LRU ** -0.5),
        'w_out': nrm(ks[14], (DEPTH, D_MODEL, D_MODEL), D_MODEL ** -0.5),
        'conv_w': nrm(ks[15], (DEPTH, CONV_W, D_LRU), CONV_W ** -0.5),
        'conv_b': nrm(ks[16], (DEPTH, D_LRU), 0.01),
        'w_rg_a': nrm(ks[17], (DEPTH, N_LRU_BLOCKS, LRU_BLOCK, LRU_BLOCK), LRU_BLOCK ** -0.5),
        'b_rg_a': nrm(ks[18], (DEPTH, D_LRU), 0.01),
        'w_rg_x': nrm(ks[19], (DEPTH, N_LRU_BLOCKS, LRU_BLOCK, LRU_BLOCK), LRU_BLOCK ** -0.5),
        'b_rg_x': nrm(ks[21], (DEPTH, D_LRU), 0.01),
        'lru_lambda': lru_lambda,
        'w_mlp1': nrm(ks[22], (DEPTH, D_MODEL, D_FF), D_MODEL ** -0.5),
        'w_mlp2': nrm(ks[23], (DEPTH, D_FF, D_MODEL), D_FF ** -0.5),
        'final_g': 1.0 + nrm(ks[24], (D_MODEL,), 0.02),
    }


def reference(x_prompt, x_sample, c_prompt, c_sample, state_ret, state_lru, state_conv,
              w_ada, b_ada, norm1_g, norm2_g, w_in, w_proj_ret, w_proj_lru, w_out,
              conv_w, conv_b, w_rg_a, b_rg_a, w_rg_x, b_rg_x, lru_lambda,
              w_mlp1, w_mlp2, final_g):
    f32 = jnp.float32
    bp, lp = x_prompt.shape[0], x_prompt.shape[1]
    pos_p = jnp.arange(lp, dtype=jnp.int32)
    pos_s = PAST_LEN + jnp.arange(x_sample.shape[1], dtype=jnp.int32)
    ret0 = jnp.zeros((DEPTH, bp, H_RET, DK_RET, DV_RET), f32)
    lru0 = jnp.zeros((DEPTH, bp, D_LRU), f32)
    conv0 = jnp.zeros((DEPTH, bp, CONV_W - 1, D_LRU), x_prompt.dtype)
    y_prompt, ret_p, lru_p, conv_p = trunk(
        x_prompt, c_prompt, pos_p, ret0, lru0, conv0, w_ada, b_ada, norm1_g, norm2_g, w_in,
        w_proj_ret, w_proj_lru, w_out, conv_w, conv_b, w_rg_a, b_rg_a, w_rg_x, b_rg_x,
        lru_lambda, w_mlp1, w_mlp2, final_g)
    y_sample, ret_s, lru_s, conv_s = trunk(
        x_sample, c_sample, pos_s, state_ret, state_lru, state_conv, w_ada, b_ada, norm1_g,
        norm2_g, w_in, w_proj_ret, w_proj_lru, w_out, conv_w, conv_b, w_rg_a, b_rg_a, w_rg_x,
        b_rg_x, lru_lambda, w_mlp1, w_mlp2, final_g)
    return (y_prompt, y_sample, ret_p, lru_p, conv_p, ret_s, lru_s, conv_s)
```

```python
import functools

import jax
import jax.numpy as jnp
from jax import lax
from jax.experimental import pallas as pl
from jax.experimental.pallas import tpu as pltpu

F32 = jnp.float32
BF16 = jnp.bfloat16

D_MODEL = 1024
DEPTH = 2
PAST_LEN = 16384
H_RET = 4
DK_RET = D_MODEL // 8
DV_RET = D_MODEL // 4
D_RET_QK = H_RET * DK_RET
D_RET_V = H_RET * DV_RET
RET_CHUNK = 128
ROPE_BASE = 10000.0
D_LRU = D_MODEL
N_LRU_BLOCKS = 8
LRU_BLOCK = D_LRU // N_LRU_BLOCKS
CONV_W = 4
RG_C = 8.0
D_FF = 4 * D_MODEL
NORM_EPS = 1e-6
GN_EPS = 1e-5
D_IN = 2 * D_RET_QK + 2 * D_RET_V + 2 * D_LRU + 2 * D_MODEL

OFF_Q = 0
OFF_K = D_RET_QK
OFF_V = 2 * D_RET_QK
OFF_G = OFF_V + D_RET_V
OFF_XL = OFF_G + D_RET_V
OFF_GL = OFF_XL + D_LRU
OFF_MA = OFF_GL + D_LRU
OFF_MB = OFF_MA + D_MODEL

SUBLANES = 8
VMEM_LIMIT = 60 * 1024 * 1024

PROMPT_TILE = 256
MLP_ROWS = 512
SAMPLE_PROJ_SEQS = 32
SAMPLE_RET_SEQS = 8
SAMPLE_POST_SEQS = 32


def _dot(a, b):
    return jnp.dot(a, b, preferred_element_type=F32)


def _dot_nt(a, b):
    return lax.dot_general(a, b, (((1,), (1,)), ((), ())), preferred_element_type=F32)


def _dot_tn(a, b):
    return lax.dot_general(a, b, (((0,), (0,)), ((), ())), preferred_element_type=F32)


def _rmsnorm(x, g):
    ms = jnp.mean(x * x, axis=-1, keepdims=True)
    return (x * lax.rsqrt(ms + NORM_EPS)) * g


def _silu(x):
    return x * jax.nn.sigmoid(x)


def _rope(x, cos, sin_signed):
    return x * cos + pltpu.roll(x, DK_RET // 2, axis=1) * sin_signed


def _groupnorm(o):
    mu = jnp.mean(o, axis=-1, keepdims=True)
    d = o - mu
    var = jnp.mean(d * d, axis=-1, keepdims=True)
    return d * lax.rsqrt(var + GN_EPS)


def _lru_gates(xc, wa_ref, ba, wx_ref, bx, lam, a_s, u_s):
    neg = -lam
    softplus = jnp.maximum(neg, 0.0) + jnp.log1p(jnp.exp(-jnp.abs(neg)))
    for n in range(N_LRU_BLOCKS):
        cols = slice(n * LRU_BLOCK, (n + 1) * LRU_BLOCK)
        blk = xc[:, cols]
        blkb = blk.astype(BF16)
        r = jax.nn.sigmoid(_dot(blkb, wa_ref[n]) + ba[:, cols])
        i = jax.nn.sigmoid(_dot(blkb, wx_ref[n]) + bx[:, cols])
        a = jnp.exp((-RG_C) * r * softplus[:, cols])
        a_s[n] = a
        u_s[n] = jnp.sqrt(1.0 - a * a) * (i * blk)


def _merge_out(ma, mb, p_ret, p_lru, w_out_ref):
    m = jax.nn.sigmoid(ma) * p_ret + jax.nn.sigmoid(mb) * p_lru
    return _dot(m.astype(BF16), w_out_ref[...])


def _ada_body(c_ref, w_ref, b_ref, o_ref):
    s = _silu(c_ref[...]).astype(BF16)
    o_ref[...] = _dot(s, w_ref[...].astype(BF16)) + b_ref[...]


def _ada_call(c_all, w_ada, b_ada):
    n = c_all.shape[0]
    tn = D_MODEL
    return pl.pallas_call(
        _ada_body,
        grid=(DEPTH, 6 * D_MODEL // tn),
        in_specs=[
            pl.BlockSpec((n, D_MODEL), lambda l, j: (0, 0)),
            pl.BlockSpec((None, D_MODEL, tn), lambda l, j: (l, 0, j)),
            pl.BlockSpec((None, 1, tn), lambda l, j: (l, 0, j)),
        ],
        out_specs=pl.BlockSpec((None, n, tn), lambda l, j: (l, 0, j)),
        out_shape=jax.ShapeDtypeStruct((DEPTH, n, 6 * D_MODEL), F32),
        compiler_params=pltpu.CompilerParams(
            dimension_semantics=("arbitrary", "arbitrary"), vmem_limit_bytes=VMEM_LIMIT),
        name="ada",
    )(c_all, w_ada, b_ada.reshape(DEPTH, 1, 6 * D_MODEL))


def _prompt_mixer_body(cd_ref, x_ref, sh_ref, sc_ref, gt_ref, g1_ref, cos_ref, sin_ref,
                       intra_ref, qd_ref, kd_ref, w_in_ref, w_pr_ref, w_pl_ref, w_out_ref,
                       cw_ref, cb_ref, wa_ref, ba_ref, wx_ref, bx_ref, lam_ref,
                       xo_ref, sret_ref, slru_ref, sconv_ref,
                       xpad_s, a_s, u_s, carry_s, o_s):
    t = pl.program_id(1)
    T = x_ref.shape[0]
    C = RET_CHUNK

    @pl.when(t == 0)
    def _():
        sret_ref[...] = jnp.zeros_like(sret_ref)
        slru_ref[...] = jnp.zeros_like(slru_ref)
        xpad_s[0:SUBLANES, :] = jnp.zeros((SUBLANES, D_LRU), F32)

    x = x_ref[...]
    h = _rmsnorm(x, g1_ref[...]) * (1.0 + sc_ref[...]) + sh_ref[...]
    hb = h.astype(BF16)

    qk = _dot(hb, w_in_ref[:, OFF_Q:OFF_V])
    v = _dot(hb, w_in_ref[:, OFF_V:OFF_G])
    for c in range(T // C):
        rows = slice(c * C, (c + 1) * C)
        cos = cos_ref[rows, :]
        sin = sin_ref[rows, :]
        for hd in range(H_RET):
            qr = _rope(qk[rows, OFF_Q + hd * DK_RET:OFF_Q + (hd + 1) * DK_RET], cos, sin)
            kr = _rope(qk[rows, OFF_K + hd * DK_RET:OFF_K + (hd + 1) * DK_RET], cos, sin)
            kr = kr * (DK_RET ** -0.5)
            vb = v[rows, hd * DV_RET:(hd + 1) * DV_RET].astype(BF16)
            s_old = sret_ref[hd]
            scores = _dot_nt(qr.astype(BF16), kr.astype(BF16)) * intra_ref[hd]
            o = (_dot(scores.astype(BF16), vb)
                 + _dot((qr * qd_ref[hd]).astype(BF16), s_old.astype(BF16)))
            sret_ref[hd] = (s_old * cd_ref[hd]
                            + _dot_tn((kr * kd_ref[hd]).astype(BF16), vb))
            o_s[rows, hd * DV_RET:(hd + 1) * DV_RET] = _groupnorm(o)
    g = _dot(hb, w_in_ref[:, OFF_G:OFF_XL])
    p_ret = _dot((o_s[...] * _silu(g)).astype(BF16), w_pr_ref[...])

    xl = _dot(hb, w_in_ref[:, OFF_XL:OFF_GL])
    xpad_s[SUBLANES:SUBLANES + T, :] = xl
    xc = cw_ref[CONV_W - 1:CONV_W, :] * xl + cb_ref[...]
    for k in range(1, CONV_W):
        xc = xc + cw_ref[CONV_W - 1 - k:CONV_W - k, :] * xpad_s[SUBLANES - k:SUBLANES - k + T, :]
    tail = xpad_s[T:T + SUBLANES, :]
    sconv_ref[...] = tail
    xpad_s[0:SUBLANES, :] = tail

    _lru_gates(xc, wa_ref, ba_ref[...], wx_ref, bx_ref[...], lam_ref[...], a_s, u_s)
    G = T // SUBLANES
    hs = []
    for n in range(N_LRU_BLOCKS):
        cols = slice(n * LRU_BLOCK, (n + 1) * LRU_BLOCK)
        hloc = jnp.zeros((SUBLANES, LRU_BLOCK), F32)
        acum = jnp.ones((SUBLANES, LRU_BLOCK), F32)
        for i in range(G):
            idx = pl.ds(i, SUBLANES, stride=G)
            ai = a_s[n, idx, :]
            hloc = ai * hloc + u_s[n, idx, :]
            acum = acum * ai
            u_s[n, idx, :] = hloc
            a_s[n, idx, :] = acum
        carry = slru_ref[:, cols]
        for j in range(SUBLANES):
            carry_s[j:j + 1, cols] = carry
            carry = hloc[j:j + 1, :] + acum[j:j + 1, :] * carry
        slru_ref[:, cols] = carry
        parts = []
        for j in range(SUBLANES):
            rows = slice(j * G, (j + 1) * G)
            parts.append(u_s[n, rows, :] + a_s[n, rows, :] * carry_s[j:j + 1, cols])
        hs.append(jnp.concatenate(parts, axis=0))
    gl = _dot(hb, w_in_ref[:, OFF_GL:OFF_MA])
    y = jnp.concatenate(hs, axis=1) * jax.nn.gelu(gl)
    p_lru = _dot(y.astype(BF16), w_pl_ref[...])

    ma = _dot(hb, w_in_ref[:, OFF_MA:OFF_MB])
    mb = _dot(hb, w_in_ref[:, OFF_MB:D_IN])
    mix = _merge_out(ma, mb, p_ret, p_lru, w_out_ref)
    xo_ref[...] = x + gt_ref[...] * mix


def _const_spec(shape):
    nd = len(shape)
    return pl.BlockSpec(shape, lambda *_: (0,) * nd, pipeline_mode=pl.Buffered(1))


def _prompt_mixer_call(x, ada, g1, tables, w):
    B, L, _ = x.shape
    T = PROMPT_TILE
    cos, sin, intra, qd, kd, cd = tables
    mod = lambda k: pl.BlockSpec((None, None, 1, D_MODEL), lambda b, t: (b, k, 0, 0))
    in_specs = [
        pl.BlockSpec(memory_space=pltpu.SMEM),
        pl.BlockSpec((None, T, D_MODEL), lambda b, t: (b, t, 0)),
        mod(0), mod(1), mod(2),
        _const_spec((1, D_MODEL)),
        pl.BlockSpec((T, DK_RET), lambda b, t: (t, 0)),
        pl.BlockSpec((T, DK_RET), lambda b, t: (t, 0)),
        _const_spec(intra.shape), _const_spec(qd.shape), _const_spec(kd.shape),
        _const_spec((D_MODEL, D_IN)), _const_spec((D_RET_V, D_MODEL)),
        _const_spec((D_LRU, D_MODEL)), _const_spec((D_MODEL, D_MODEL)),
        _const_spec((CONV_W, D_LRU)), _const_spec((1, D_LRU)),
        _const_spec((N_LRU_BLOCKS, LRU_BLOCK, LRU_BLOCK)), _const_spec((1, D_LRU)),
        _const_spec((N_LRU_BLOCKS, LRU_BLOCK, LRU_BLOCK)), _const_spec((1, D_LRU)),
        _const_spec((1, D_LRU)),
    ]
    out_specs = [
        pl.BlockSpec((None, T, D_MODEL), lambda b, t: (b, t, 0)),
        pl.BlockSpec((None, H_RET, DK_RET, DV_RET), lambda b, t: (b, 0, 0, 0)),
        pl.BlockSpec((None, 1, D_LRU), lambda b, t: (b, 0, 0)),
        pl.BlockSpec((None, SUBLANES, D_LRU), lambda b, t: (b, 0, 0)),
    ]
    out_shape = [
        jax.ShapeDtypeStruct((B, L, D_MODEL), F32),
        jax.ShapeDtypeStruct((B, H_RET, DK_RET, DV_RET), F32),
        jax.ShapeDtypeStruct((B, 1, D_LRU), F32),
        jax.ShapeDtypeStruct((B, SUBLANES, D_LRU), F32),
    ]
    scratch = [
        pltpu.VMEM((T + SUBLANES, D_LRU), F32),
        pltpu.VMEM((N_LRU_BLOCKS, T, LRU_BLOCK), F32),
        pltpu.VMEM((N_LRU_BLOCKS, T, LRU_BLOCK), F32),
        pltpu.VMEM((SUBLANES, D_LRU), F32),
        pltpu.VMEM((T, D_RET_V), F32),
    ]
    return pl.pallas_call(
        _prompt_mixer_body,
        grid=(B, L // T),
        in_specs=in_specs, out_specs=out_specs, out_shape=out_shape,
        scratch_shapes=scratch,
        compiler_params=pltpu.CompilerParams(
            dimension_semantics=("arbitrary", "arbitrary"), vmem_limit_bytes=VMEM_LIMIT),
        name="prompt_mixer",
    )(cd, x, ada, ada, ada, g1, cos, sin, intra, qd, kd,
      w["w_in"], w["w_proj_ret"], w["w_proj_lru"], w["w_out"],
      w["conv_w"], w["conv_b"], w["w_rg_a"], w["b_rg_a"], w["w_rg_x"], w["b_rg_x"],
      w["lru_lambda"])


def _mlp_body(x_ref, sh_ref, sc_ref, gt_ref, g2_ref, w1_ref, w2_ref, gf_ref, o_ref, *, final):
    x = x_ref[...]
    nb, tl, _ = x.shape
    h = _rmsnorm(x, g2_ref[...]) * (1.0 + sc_ref[...]) + sh_ref[...]
    hb = h.reshape(nb * tl, D_MODEL).astype(BF16)
    acc = jnp.zeros((nb * tl, D_MODEL), F32)
    for c in range(D_FF // D_MODEL):
        cols = slice(c * D_MODEL, (c + 1) * D_MODEL)
        f = jnp.maximum(_dot(hb, w1_ref[:, cols]), 0.0)
        acc = acc + _dot((f * f).astype(BF16), w2_ref[cols, :])
    y = x + gt_ref[...] * acc.reshape(nb, tl, D_MODEL)
    if final:
        y = _rmsnorm(y, gf_ref[...])
    o_ref[...] = y


def _mlp_call(x, ada, g2, w1, w2, gf, final):
    B, L, _ = x.shape
    if L >= MLP_ROWS:
        nb, tl = 1, MLP_ROWS
    else:
        nb, tl = MLP_ROWS // L, L
    mod = lambda k: pl.BlockSpec((nb, None, 1, D_MODEL), lambda b, t: (b, k, 0, 0))
    xspec = pl.BlockSpec((nb, tl, D_MODEL), lambda b, t: (b, t, 0))
    return pl.pallas_call(
        functools.partial(_mlp_body, final=final),
        grid=(B // nb, L // tl),
        in_specs=[xspec, mod(3), mod(4), mod(5), _const_spec((1, D_MODEL)),
                  _const_spec((D_MODEL, D_FF)), _const_spec((D_FF, D_MODEL)),
                  _const_spec((1, D_MODEL))],
        out_specs=xspec,
        out_shape=jax.ShapeDtypeStruct(x.shape, F32),
        compiler_params=pltpu.CompilerParams(
            dimension_semantics=("arbitrary", "arbitrary"), vmem_limit_bytes=VMEM_LIMIT),
        name="mlp",
    )(x, ada, ada, ada, g2, w1, w2, gf)


def _sample_proj_body(x_ref, sh_ref, sc_ref, g1_ref, cos_ref, sin_ref, w_in_ref, z_ref):
    x = x_ref[...]
    nb, tl, _ = x.shape
    h = _rmsnorm(x, g1_ref[...]) * (1.0 + sc_ref[...]) + sh_ref[...]
    hb = h.reshape(nb * tl, D_MODEL).astype(BF16)
    cos = cos_ref[...]
    sin = sin_ref[...]
    qk = _dot(hb, w_in_ref[:, OFF_Q:OFF_V])
    for hd in range(H_RET):
        cq = slice(OFF_Q + hd * DK_RET, OFF_Q + (hd + 1) * DK_RET)
        ck = slice(OFF_K + hd * DK_RET, OFF_K + (hd + 1) * DK_RET)
        z_ref[:, cq] = _rope(qk[:, cq], cos, sin)
        z_ref[:, ck] = _rope(qk[:, ck], cos, sin) * (DK_RET ** -0.5)
    for off in range(OFF_V, D_IN, D_MODEL):
        z_ref[:, off:off + D_MODEL] = _dot(hb, w_in_ref[:, off:off + D_MODEL])


def _sample_proj_call(x, ada, g1, cos, sin, w_in):
    B, L, _ = x.shape
    nb = SAMPLE_PROJ_SEQS
    mod = lambda k: pl.BlockSpec((nb, None, 1, D_MODEL), lambda i: (i, k, 0, 0))
    return pl.pallas_call(
        _sample_proj_body,
        grid=(B // nb,),
        in_specs=[pl.BlockSpec((nb, L, D_MODEL), lambda i: (i, 0, 0)), mod(0), mod(1),
                  _const_spec((1, D_MODEL)), _const_spec((nb * L, DK_RET)),
                  _const_spec((nb * L, DK_RET)), _const_spec((D_MODEL, D_IN))],
        out_specs=pl.BlockSpec((nb * L, D_IN), lambda i: (i, 0)),
        out_shape=jax.ShapeDtypeStruct((B * L, D_IN), F32),
        compiler_params=pltpu.CompilerParams(
            dimension_semantics=("arbitrary",), vmem_limit_bytes=VMEM_LIMIT),
        name="sample_proj",
    )(x, ada, ada, g1, cos, sin, w_in)


def _sample_ret_body(cd_ref, qk_ref, v_ref, s_ref, mask_ref, qd_ref, kd_ref, o_ref, sn_ref, *, L):
    nb = s_ref.shape[0]
    for hd in range(H_RET):
        q = qk_ref[:, OFF_Q + hd * DK_RET:OFF_Q + (hd + 1) * DK_RET]
        k = qk_ref[:, OFF_K + hd * DK_RET:OFF_K + (hd + 1) * DK_RET]
        v = v_ref[:, hd * DV_RET:(hd + 1) * DV_RET]
        vb = v.astype(BF16)
        scores = _dot_nt(q.astype(BF16), k.astype(BF16)) * mask_ref[hd]
        o_intra = _dot(scores.astype(BF16), vb)
        qs = q * qd_ref[hd]
        ks = k * kd_ref[hd]
        o_inter = []
        for b in range(nb):
            rows = slice(b * L, (b + 1) * L)
            s_old = s_ref[b, hd]
            o_inter.append(_dot(qs[rows].astype(BF16), s_old.astype(BF16)))
            sn_ref[b, hd] = (s_old * cd_ref[hd]
                             + _dot_tn(ks[rows].astype(BF16), v[rows].astype(BF16)))
        o = o_intra + jnp.concatenate(o_inter, axis=0)
        o_ref[:, hd * DV_RET:(hd + 1) * DV_RET] = _groupnorm(o)


def _sample_ret_call(z, state, layer, tables, L):
    nb = SAMPLE_RET_SEQS
    B = state.shape[1]
    R = nb * L
    mask, qd, kd, cd = tables
    sspec_in = pl.BlockSpec((None, nb, H_RET, DK_RET, DV_RET), lambda i: (layer, i, 0, 0, 0))
    return pl.pallas_call(
        functools.partial(_sample_ret_body, L=L),
        grid=(B // nb,),
        in_specs=[pl.BlockSpec(memory_space=pltpu.SMEM),
                  pl.BlockSpec((R, 2 * D_RET_QK), lambda i: (i, 0)),
                  pl.BlockSpec((R, D_RET_V), lambda i: (i, OFF_V // D_RET_V)),
                  sspec_in,
                  _const_spec(mask.shape), _const_spec(qd.shape), _const_spec(kd.shape)],
        out_specs=[pl.BlockSpec((R, D_RET_V), lambda i: (i, 0)),
                   pl.BlockSpec((nb, H_RET, DK_RET, DV_RET), lambda i: (i, 0, 0, 0))],
        out_shape=[jax.ShapeDtypeStruct((B * L, D_RET_V), F32),
                   jax.ShapeDtypeStruct((B, H_RET, DK_RET, DV_RET), F32)],
        compiler_params=pltpu.CompilerParams(
            dimension_semantics=("arbitrary",), vmem_limit_bytes=VMEM_LIMIT),
        name="sample_ret",
    )(cd, z, z, state, mask, qd, kd)


def _sample_post_body(x_ref, gt_ref, o_ref, g_ref, xl_ref, gl_ref, ma_ref, mb_ref,
                      conv_ref, lru_ref, w_pr_ref, w_pl_ref, w_out_ref,
                      cw_ref, cb_ref, wa_ref, ba_ref, wx_ref, bx_ref, lam_ref,
                      xo_ref, lruo_ref, a_s, u_s):
    x = x_ref[...]
    nb, L, _ = x.shape
    R = nb * L
    p_ret = _dot((o_ref[...] * _silu(g_ref[...])).astype(BF16), w_pr_ref[...])

    xl3 = xl_ref[...].reshape(nb, L, D_LRU)
    buf = conv_ref[...]
    row = lax.broadcasted_iota(jnp.int32, (nb, L, D_LRU), 1)
    xc3 = cw_ref[CONV_W - 1:CONV_W, :] * xl3 + cb_ref[...]
    for k in range(1, CONV_W):
        src = jnp.where(row >= L - k, buf, xl3)
        xc3 = xc3 + cw_ref[CONV_W - 1 - k:CONV_W - k, :] * pltpu.roll(src, k, axis=1)
    xc = xc3.reshape(R, D_LRU)

    _lru_gates(xc, wa_ref, ba_ref[...], wx_ref, bx_ref[...], lam_ref[...], a_s, u_s)
    hs = []
    for n in range(N_LRU_BLOCKS):
        cols = slice(n * LRU_BLOCK, (n + 1) * LRU_BLOCK)
        hcur = lru_ref[:, cols]
        for t in range(L):
            idx = pl.ds(t, nb, stride=L)
            hcur = a_s[n, idx, :] * hcur + u_s[n, idx, :]
            u_s[n, idx, :] = hcur
        lruo_ref[:, cols] = hcur
        hs.append(u_s[n])
    y = jnp.concatenate(hs, axis=1) * jax.nn.gelu(gl_ref[...])
    p_lru = _dot(y.astype(BF16), w_pl_ref[...])

    mix = _merge_out(ma_ref[...], mb_ref[...], p_ret, p_lru, w_out_ref)
    xo_ref[...] = x + gt_ref[...] * mix.reshape(nb, L, D_MODEL)


def _sample_post_call(x, ada, o, z, conv_pad, state_lru, layer, w):
    B, L, _ = x.shape
    nb = SAMPLE_POST_SEQS
    R = nb * L
    zcol = lambda off: pl.BlockSpec((R, D_MODEL), lambda i: (i, off // D_MODEL))
    return pl.pallas_call(
        _sample_post_body,
        grid=(B // nb,),
        in_specs=[
            pl.BlockSpec((nb, L, D_MODEL), lambda i: (i, 0, 0)),
            pl.BlockSpec((nb, None, 1, D_MODEL), lambda i: (i, 2, 0, 0)),
            pl.BlockSpec((R, D_RET_V), lambda i: (i, 0)),
            zcol(OFF_G), zcol(OFF_XL), zcol(OFF_GL), zcol(OFF_MA), zcol(OFF_MB),
            pl.BlockSpec((None, nb, SUBLANES, D_LRU), lambda i: (layer, i, 0, 0)),
            pl.BlockSpec((None, nb, D_LRU), lambda i: (layer, i, 0)),
            _const_spec((D_RET_V, D_MODEL)), _const_spec((D_LRU, D_MODEL)),
            _const_spec((D_MODEL, D_MODEL)),
            _const_spec((CONV_W, D_LRU)), _const_spec((1, D_LRU)),
            _const_spec((N_LRU_BLOCKS, LRU_BLOCK, LRU_BLOCK)), _const_spec((1, D_LRU)),
            _const_spec((N_LRU_BLOCKS, LRU_BLOCK, LRU_BLOCK)), _const_spec((1, D_LRU)),
            _const_spec((1, D_LRU)),
        ],
        out_specs=[pl.BlockSpec((nb, L, D_MODEL), lambda i: (i, 0, 0)),
                   pl.BlockSpec((nb, D_LRU), lambda i: (i, 0))],
        out_shape=[jax.ShapeDtypeStruct(x.shape, F32),
                   jax.ShapeDtypeStruct((B, D_LRU), F32)],
        scratch_shapes=[pltpu.VMEM((N_LRU_BLOCKS, R, LRU_BLOCK), F32),
                        pltpu.VMEM((N_LRU_BLOCKS, R, LRU_BLOCK), F32)],
        compiler_params=pltpu.CompilerParams(
            dimension_semantics=("arbitrary",), vmem_limit_bytes=VMEM_LIMIT),
        name="sample_post",
    )(x, ada, o, z, z, z, z, z, conv_pad, state_lru,
      w["w_proj_ret"], w["w_proj_lru"], w["w_out"],
      w["conv_w"], w["conv_b"], w["w_rg_a"], w["b_rg_a"], w["w_rg_x"], w["b_rg_x"],
      w["lru_lambda"])


def _rope_tables(pos):
    half = DK_RET // 2
    inv = ROPE_BASE ** (-jnp.arange(half, dtype=F32) / half)
    ang = pos.astype(F32)[:, None] * inv[None, :]
    cos = jnp.cos(ang)
    sin = jnp.sin(ang)
    return jnp.concatenate([cos, cos], axis=-1), jnp.concatenate([-sin, sin], axis=-1)


def _decay_tables(C):
    log_g = jnp.log1p(-jnp.exp2(-5.0 - jnp.arange(H_RET, dtype=F32)))
    idx = jnp.arange(C, dtype=F32)
    diff = idx[:, None] - idx[None, :]
    intra = jnp.where(diff[None] >= 0,
                      jnp.exp(jnp.maximum(diff, 0.0)[None] * log_g[:, None, None]), 0.0)
    q_decay = jnp.exp((idx + 1.0)[None, :] * log_g[:, None])
    k_decay = jnp.exp((C - 1.0 - idx)[None, :] * log_g[:, None])
    chunk_decay = jnp.exp(C * log_g)
    return intra, q_decay, k_decay, chunk_decay


def _lanes(t, reps=1):
    t = jnp.tile(t, (1, reps))
    return jnp.broadcast_to(t[:, :, None], t.shape + (DK_RET,))


def kernel(x_prompt, x_sample, c_prompt, c_sample, state_ret, state_lru, state_conv, w_ada, b_ada, norm1_g, norm2_g, w_in, w_proj_ret, w_proj_lru, w_out, conv_w, conv_b, w_rg_a, b_rg_a, w_rg_x, b_rg_x, lru_lambda, w_mlp1, w_mlp2, final_g):
    bp, lp, _ = x_prompt.shape
    bs, ls, _ = x_sample.shape
    assert lp % PROMPT_TILE == 0 and PROMPT_TILE % RET_CHUNK == 0
    assert ls == SUBLANES and ls % RET_CHUNK != 0 and ls >= CONV_W - 1

    ada = _ada_call(jnp.concatenate([c_prompt, c_sample], axis=0), w_ada, b_ada)
    ada = ada.reshape(DEPTH, bp + bs, 6, 1, D_MODEL)

    row = lambda p: p.reshape(DEPTH, 1, -1)
    layers = []
    for l in range(DEPTH):
        layers.append(dict(
            w_in=w_in[l].astype(BF16), w_proj_ret=w_proj_ret[l].astype(BF16),
            w_proj_lru=w_proj_lru[l].astype(BF16), w_out=w_out[l].astype(BF16),
            w_rg_a=w_rg_a[l].astype(BF16), w_rg_x=w_rg_x[l].astype(BF16),
            w_mlp1=w_mlp1[l].astype(BF16), w_mlp2=w_mlp2[l].astype(BF16),
            conv_w=conv_w[l], conv_b=row(conv_b)[l], b_rg_a=row(b_rg_a)[l],
            b_rg_x=row(b_rg_x)[l], lru_lambda=row(lru_lambda)[l],
            g1=row(norm1_g)[l], g2=row(norm2_g)[l]))
    gf = final_g.reshape(1, D_MODEL)

    cos_p, sin_p = _rope_tables(jnp.arange(lp, dtype=jnp.int32))
    intra, qd, kd, cd = _decay_tables(RET_CHUNK)
    tables_p = (cos_p, sin_p, intra, _lanes(qd), _lanes(kd), cd)
    x = x_prompt
    ret_p, lru_p, conv_p = [], [], []
    for l, w in enumerate(layers):
        ada_l = ada[l, :bp]
        x, sr, sl, sc = _prompt_mixer_call(x, ada_l, w["g1"], tables_p, w)
        x = _mlp_call(x, ada_l, w["g2"], w["w_mlp1"], w["w_mlp2"], gf, l == DEPTH - 1)
        ret_p.append(sr)
        lru_p.append(sl[:, 0])
        conv_p.append(sc[:, SUBLANES - (CONV_W - 1):])
    y_prompt = x

    cos_s, sin_s = _rope_tables(PAST_LEN + jnp.arange(ls, dtype=jnp.int32))
    cos_s = jnp.tile(cos_s, (SAMPLE_PROJ_SEQS, 1))
    sin_s = jnp.tile(sin_s, (SAMPLE_PROJ_SEQS, 1))
    intra, qd, kd, cd = _decay_tables(ls)
    eye = jnp.eye(SAMPLE_RET_SEQS, dtype=F32)
    mask = jnp.einsum("ab,hqk->haqbk", eye, intra).reshape(
        H_RET, SAMPLE_RET_SEQS * ls, SAMPLE_RET_SEQS * ls)
    tables_s = (mask, _lanes(qd, SAMPLE_RET_SEQS), _lanes(kd, SAMPLE_RET_SEQS), cd)
    conv_pad = jnp.pad(state_conv, ((0, 0), (0, 0), (SUBLANES - (CONV_W - 1), 0), (0, 0)))
    x = x_sample
    ret_s, lru_s, conv_s = [], [], []
    for l, w in enumerate(layers):
        ada_l = ada[l, bp:]
        z = _sample_proj_call(x, ada_l, w["g1"], cos_s, sin_s, w["w_in"])
        o, sr = _sample_ret_call(z, state_ret, l, tables_s, ls)
        x, sl = _sample_post_call(x, ada_l, o, z, conv_pad, state_lru, l, w)
        x = _mlp_call(x, ada_l, w["g2"], w["w_mlp1"], w["w_mlp2"], gf, l == DEPTH - 1)
        ret_s.append(sr)
        lru_s.append(sl)
        conv_s.append(z[:, OFF_XL:OFF_GL].reshape(bs, ls, D_LRU)[:, ls - (CONV_W - 1):])
    y_sample = x

    return (y_prompt, y_sample, jnp.stack(ret_p), jnp.stack(lru_p), jnp.stack(conv_p),
            jnp.stack(ret_s), jnp.stack(lru_s), jnp.stack(conv_s))
```

```python
import functools

import jax
import jax.numpy as jnp
from jax import lax
from jax.experimental import pallas as pl
from jax.experimental.pallas import tpu as pltpu

F32 = jnp.float32
BF16 = jnp.bfloat16

D_MODEL = 1024
DEPTH = 2
PAST_LEN = 16384
H_RET = 4
DK_RET = D_MODEL // 8
DV_RET = D_MODEL // 4
D_RET_QK = H_RET * DK_RET
D_RET_V = H_RET * DV_RET
RET_CHUNK = 128
ROPE_BASE = 10000.0
D_LRU = D_MODEL
N_LRU_BLOCKS = 8
LRU_BLOCK = D_LRU // N_LRU_BLOCKS
CONV_W = 4
RG_C = 8.0
D_FF = 4 * D_MODEL
NORM_EPS = 1e-6
GN_EPS = 1e-5
D_IN = 2 * D_RET_QK + 2 * D_RET_V + 2 * D_LRU + 2 * D_MODEL

OFF_Q = 0
OFF_K = D_RET_QK
OFF_V = 2 * D_RET_QK
OFF_G = OFF_V + D_RET_V
OFF_XL = OFF_G + D_RET_V
OFF_GL = OFF_XL + D_LRU
OFF_MA = OFF_GL + D_LRU
OFF_MB = OFF_MA + D_MODEL

SUBLANES = 8
VMEM_LIMIT = 60 * 1024 * 1024

PROMPT_TILE = 256
PROJ_COLS = 256
MLP_ROWS = 512
SAMPLE_PROJ_SEQS = 32
SAMPLE_RET_SEQS = 8
SAMPLE_POST_SEQS = 32


def _dot(a, b):
    return jnp.dot(a, b, preferred_element_type=F32)


def _dot_nt(a, b):
    return lax.dot_general(a, b, (((1,), (1,)), ((), ())), preferred_element_type=F32)


def _dot_tn(a, b):
    return lax.dot_general(a, b, (((0,), (0,)), ((), ())), preferred_element_type=F32)


def _rmsnorm(x, g):
    ms = jnp.mean(x * x, axis=-1, keepdims=True)
    return (x * lax.rsqrt(ms + NORM_EPS)) * g


def _silu(x):
    return x * jax.nn.sigmoid(x)


def _rope(x, cos, sin_signed):
    return x * cos + pltpu.roll(x, DK_RET // 2, axis=1) * sin_signed


def _groupnorm(o):
    mu = jnp.mean(o, axis=-1, keepdims=True)
    d = o - mu
    var = jnp.mean(d * d, axis=-1, keepdims=True)
    return d * lax.rsqrt(var + GN_EPS)


def _softplus(y):
    return jnp.maximum(y, 0.0) + jnp.log1p(jnp.exp(-jnp.abs(y)))


def _lru_gate_block(n, xc_blk, wa_ref, ba, wx_ref, bx, softplus_neg_lam):
    cols = slice(n * LRU_BLOCK, (n + 1) * LRU_BLOCK)
    blkb = xc_blk.astype(BF16)
    r = jax.nn.sigmoid(_dot(blkb, wa_ref[n]) + ba[:, cols])
    i = jax.nn.sigmoid(_dot(blkb, wx_ref[n]) + bx[:, cols])
    a = jnp.exp((-RG_C) * r * softplus_neg_lam[:, cols])
    return a, jnp.sqrt(1.0 - a * a) * (i * xc_blk)


def _merge_out(ma, mb, p_ret, p_lru, w_out_ref):
    m = jax.nn.sigmoid(ma) * p_ret + jax.nn.sigmoid(mb) * p_lru
    return _dot(m.astype(BF16), w_out_ref[...])


def _ada_body(c_ref, w_ref, b_ref, o_ref):
    s = _silu(c_ref[...]).astype(BF16)
    o_ref[...] = _dot(s, w_ref[...].astype(BF16)) + b_ref[...]


def _ada_call(c_all, w_ada, b_ada):
    n = c_all.shape[0]
    tn = D_MODEL
    return pl.pallas_call(
        _ada_body,
        grid=(DEPTH, 6 * D_MODEL // tn),
        in_specs=[
            pl.BlockSpec((n, D_MODEL), lambda l, j: (0, 0)),
            pl.BlockSpec((None, D_MODEL, tn), lambda l, j: (l, 0, j)),
            pl.BlockSpec((None, 1, tn), lambda l, j: (l, 0, j)),
        ],
        out_specs=pl.BlockSpec((None, n, tn), lambda l, j: (l, 0, j)),
        out_shape=jax.ShapeDtypeStruct((DEPTH, n, 6 * D_MODEL), F32),
        compiler_params=pltpu.CompilerParams(
            dimension_semantics=("arbitrary", "arbitrary"), vmem_limit_bytes=VMEM_LIMIT),
        name="ada",
    )(c_all, w_ada, b_ada.reshape(DEPTH, 1, 6 * D_MODEL))


def _prompt_mixer_body(cd_ref, x_ref, sh_ref, sc_ref, gt_ref, g1_ref, cos_ref, sin_ref,
                       intra_ref, qd_ref, kd_ref, w_in_ref, w_pr_ref, w_pl_ref, w_out_ref,
                       cw_ref, cb_ref, wa_ref, ba_ref, wx_ref, bx_ref, lam_ref,
                       xo_ref, sret_ref, slru_ref, sconv_ref,
                       hb_s, z_s, xpad_s, a_s, u_s, carry_s, o_s, y_s, pl_s):
    t = pl.program_id(1)
    T = x_ref.shape[0]
    C = RET_CHUNK
    G = T // SUBLANES
    P = G + SUBLANES

    @pl.when(t == 0)
    def _():
        sret_ref[...] = jnp.zeros_like(sret_ref)
        slru_ref[...] = jnp.zeros_like(slru_ref)
        xpad_s[0:SUBLANES, :] = jnp.zeros((SUBLANES, D_LRU), F32)

    x = x_ref[...]
    h = _rmsnorm(x, g1_ref[...]) * (1.0 + sc_ref[...]) + sh_ref[...]
    hb_s[...] = h.astype(BF16)

    def proj(off):
        cols = slice(off, off + PROJ_COLS)
        z_s[:, cols] = _dot(hb_s[...], w_in_ref[:, cols])

    def proj_lru(off):
        cols = slice(off, off + PROJ_COLS)
        pl_s[:, cols] = _dot(y_s[...], w_pl_ref[:, cols])

    for off in range(OFF_XL, OFF_GL, PROJ_COLS):
        proj(off)
    xpad_s[SUBLANES:SUBLANES + T, :] = z_s[:, OFF_XL:OFF_GL]
    sconv_ref[...] = xpad_s[T:T + SUBLANES, :]

    pending = [o for seg in (OFF_Q, OFF_V, OFF_G, OFF_GL) for o in range(seg, seg + D_MODEL, PROJ_COLS)]
    ba = ba_ref[...]
    bx = bx_ref[...]
    sp = _softplus(-lam_ref[...])
    for n in range(N_LRU_BLOCKS):
        cols = slice(n * LRU_BLOCK, (n + 1) * LRU_BLOCK)
        xc = cw_ref[CONV_W - 1:CONV_W, cols] * xpad_s[SUBLANES:SUBLANES + T, cols] + cb_ref[:, cols]
        for k in range(1, CONV_W):
            xc = xc + (cw_ref[CONV_W - 1 - k:CONV_W - k, cols]
                       * xpad_s[SUBLANES - k:SUBLANES - k + T, cols])
        a, u = _lru_gate_block(n, xc, wa_ref, ba, wx_ref, bx, sp)
        for j in range(SUBLANES):
            a_s[n, j * P:j * P + G, :] = a[j * G:(j + 1) * G]
            u_s[n, j * P:j * P + G, :] = u[j * G:(j + 1) * G]
        proj(pending.pop(0))
        hloc = jnp.zeros((SUBLANES, LRU_BLOCK), F32)
        acum = jnp.ones((SUBLANES, LRU_BLOCK), F32)
        for i in range(G):
            idx = pl.ds(i, SUBLANES, stride=P)
            ai = a_s[n, idx, :]
            hloc = ai * hloc + u_s[n, idx, :]
            acum = acum * ai
            u_s[n, idx, :] = hloc
            a_s[n, idx, :] = acum
        carry = slru_ref[:, cols]
        for j in range(SUBLANES):
            carry_s[j:j + 1, cols] = carry
            carry = hloc[j:j + 1, :] + acum[j:j + 1, :] * carry
        slru_ref[:, cols] = carry
        proj(pending.pop(0))
    xpad_s[0:SUBLANES, :] = xpad_s[T:T + SUBLANES, :]
    for n in range(N_LRU_BLOCKS):
        cols = slice(n * LRU_BLOCK, (n + 1) * LRU_BLOCK)
        for j in range(SUBLANES):
            rows = slice(j * G, (j + 1) * G)
            prow = slice(j * P, j * P + G)
            hseq = u_s[n, prow, :] + a_s[n, prow, :] * carry_s[j:j + 1, cols]
            y_s[rows, cols] = (hseq * jax.nn.gelu(z_s[rows, OFF_GL + n * LRU_BLOCK:
                                                      OFF_GL + (n + 1) * LRU_BLOCK])).astype(BF16)

    pending = [(proj, o) for o in range(OFF_MA, D_IN, PROJ_COLS)]
    pending += [(proj_lru, o) for o in range(0, D_MODEL, PROJ_COLS)]
    n_bodies = (T // C) * H_RET
    for c in range(T // C):
        rows = slice(c * C, (c + 1) * C)
        cos = cos_ref[rows, :]
        sin = sin_ref[rows, :]
        for hd in range(H_RET):
            cq = slice(OFF_Q + hd * DK_RET, OFF_Q + (hd + 1) * DK_RET)
            ck = slice(OFF_K + hd * DK_RET, OFF_K + (hd + 1) * DK_RET)
            cv = slice(OFF_V + hd * DV_RET, OFF_V + (hd + 1) * DV_RET)
            qr = _rope(z_s[rows, cq], cos, sin)
            kr = _rope(z_s[rows, ck], cos, sin) * (DK_RET ** -0.5)
            vb = z_s[rows, cv].astype(BF16)
            s_old = sret_ref[hd]
            scores = _dot_nt(qr.astype(BF16), kr.astype(BF16)) * intra_ref[hd]
            o = (_dot(scores.astype(BF16), vb)
                 + _dot((qr * qd_ref[hd]).astype(BF16), s_old.astype(BF16)))
            sret_ref[hd] = (s_old * cd_ref[hd]
                            + _dot_tn((kr * kd_ref[hd]).astype(BF16), vb))
            co = slice(hd * DV_RET, (hd + 1) * DV_RET)
            g = z_s[rows, OFF_G + hd * DV_RET:OFF_G + (hd + 1) * DV_RET]
            o_s[rows, co] = (_groupnorm(o) * _silu(g)).astype(BF16)
            body = c * H_RET + hd
            for fn, off in pending[body * len(pending) // n_bodies:
                                   (body + 1) * len(pending) // n_bodies]:
                fn(off)

    p_ret = _dot(o_s[...], w_pr_ref[...])
    m = (jax.nn.sigmoid(z_s[:, OFF_MA:OFF_MB]) * p_ret
         + jax.nn.sigmoid(z_s[:, OFF_MB:D_IN]) * pl_s[...])
    mix = _dot(m.astype(BF16), w_out_ref[...])
    xo_ref[...] = x + gt_ref[...] * mix


def _const_spec(shape):
    nd = len(shape)
    return pl.BlockSpec(shape, lambda *_: (0,) * nd, pipeline_mode=pl.Buffered(1))


def _prompt_mixer_call(x, ada, g1, tables, w):
    B, L, _ = x.shape
    T = PROMPT_TILE
    cos, sin, intra, qd, kd, cd = tables
    mod = lambda k: pl.BlockSpec((None, None, 1, D_MODEL), lambda b, t: (b, k, 0, 0))
    in_specs = [
        pl.BlockSpec(memory_space=pltpu.SMEM),
        pl.BlockSpec((None, T, D_MODEL), lambda b, t: (b, t, 0)),
        mod(0), mod(1), mod(2),
        _const_spec((1, D_MODEL)),
        pl.BlockSpec((T, DK_RET), lambda b, t: (t, 0)),
        pl.BlockSpec((T, DK_RET), lambda b, t: (t, 0)),
        _const_spec(intra.shape), _const_spec(qd.shape), _const_spec(kd.shape),
        _const_spec((D_MODEL, D_IN)), _const_spec((D_RET_V, D_MODEL)),
        _const_spec((D_LRU, D_MODEL)), _const_spec((D_MODEL, D_MODEL)),
        _const_spec((CONV_W, D_LRU)), _const_spec((1, D_LRU)),
        _const_spec((N_LRU_BLOCKS, LRU_BLOCK, LRU_BLOCK)), _const_spec((1, D_LRU)),
        _const_spec((N_LRU_BLOCKS, LRU_BLOCK, LRU_BLOCK)), _const_spec((1, D_LRU)),
        _const_spec((1, D_LRU)),
    ]
    out_specs = [
        pl.BlockSpec((None, T, D_MODEL), lambda b, t: (b, t, 0)),
        pl.BlockSpec((None, H_RET, DK_RET, DV_RET), lambda b, t: (b, 0, 0, 0)),
        pl.BlockSpec((None, 1, D_LRU), lambda b, t: (b, 0, 0)),
        pl.BlockSpec((None, SUBLANES, D_LRU), lambda b, t: (b, 0, 0)),
    ]
    out_shape = [
        jax.ShapeDtypeStruct((B, L, D_MODEL), F32),
        jax.ShapeDtypeStruct((B, H_RET, DK_RET, DV_RET), F32),
        jax.ShapeDtypeStruct((B, 1, D_LRU), F32),
        jax.ShapeDtypeStruct((B, SUBLANES, D_LRU), F32),
    ]
    scan_rows = SUBLANES * (T // SUBLANES + SUBLANES)
    scratch = [
        pltpu.VMEM((T, D_MODEL), BF16),
        pltpu.VMEM((T, D_IN), F32),
        pltpu.VMEM((T + SUBLANES, D_LRU), F32),
        pltpu.VMEM((N_LRU_BLOCKS, scan_rows, LRU_BLOCK), F32),
        pltpu.VMEM((N_LRU_BLOCKS, scan_rows, LRU_BLOCK), F32),
        pltpu.VMEM((SUBLANES, D_LRU), F32),
        pltpu.VMEM((T, D_RET_V), BF16),
        pltpu.VMEM((T, D_LRU), BF16),
        pltpu.VMEM((T, D_MODEL), F32),
    ]
    return pl.pallas_call(
        _prompt_mixer_body,
        grid=(B, L // T),
        in_specs=in_specs, out_specs=out_specs, out_shape=out_shape,
        scratch_shapes=scratch,
        compiler_params=pltpu.CompilerParams(
            dimension_semantics=("arbitrary", "arbitrary"), vmem_limit_bytes=VMEM_LIMIT),
        name="prompt_mixer",
    )(cd, x, ada, ada, ada, g1, cos, sin, intra, qd, kd,
      w["w_in"], w["w_proj_ret"], w["w_proj_lru"], w["w_out"],
      w["conv_w"], w["conv_b"], w["w_rg_a"], w["b_rg_a"], w["w_rg_x"], w["b_rg_x"],
      w["lru_lambda"])


def _mlp_body(x_ref, sh_ref, sc_ref, gt_ref, g2_ref, w1_ref, w2_ref, gf_ref, o_ref, *, final):
    x = x_ref[...]
    nb, tl, _ = x.shape
    h = _rmsnorm(x, g2_ref[...]) * (1.0 + sc_ref[...]) + sh_ref[...]
    hb = h.reshape(nb * tl, D_MODEL).astype(BF16)
    acc = jnp.zeros((nb * tl, D_MODEL), F32)
    for c in range(D_FF // D_MODEL):
        cols = slice(c * D_MODEL, (c + 1) * D_MODEL)
        f = jnp.maximum(_dot(hb, w1_ref[:, cols]), 0.0)
        acc = acc + _dot((f * f).astype(BF16), w2_ref[cols, :])
    y = x + gt_ref[...] * acc.reshape(nb, tl, D_MODEL)
    if final:
        y = _rmsnorm(y, gf_ref[...])
    o_ref[...] = y


def _mlp_call(x, ada, g2, w1, w2, gf, final):
    B, L, _ = x.shape
    if L >= MLP_ROWS:
        nb, tl = 1, MLP_ROWS
    else:
        nb, tl = MLP_ROWS // L, L
    mod = lambda k: pl.BlockSpec((nb, None, 1, D_MODEL), lambda b, t: (b, k, 0, 0))
    xspec = pl.BlockSpec((nb, tl, D_MODEL), lambda b, t: (b, t, 0))
    return pl.pallas_call(
        functools.partial(_mlp_body, final=final),
        grid=(B // nb, L // tl),
        in_specs=[xspec, mod(3), mod(4), mod(5), _const_spec((1, D_MODEL)),
                  _const_spec((D_MODEL, D_FF)), _const_spec((D_FF, D_MODEL)),
                  _const_spec((1, D_MODEL))],
        out_specs=xspec,
        out_shape=jax.ShapeDtypeStruct(x.shape, F32),
        compiler_params=pltpu.CompilerParams(
            dimension_semantics=("arbitrary", "arbitrary"), vmem_limit_bytes=VMEM_LIMIT),
        name="mlp",
    )(x, ada, ada, ada, g2, w1, w2, gf)


def _sample_proj_body(x_ref, sh_ref, sc_ref, g1_ref, cos_ref, sin_ref, w_in_ref, z_ref):
    x = x_ref[...]
    nb, tl, _ = x.shape
    h = _rmsnorm(x, g1_ref[...]) * (1.0 + sc_ref[...]) + sh_ref[...]
    hb = h.reshape(nb * tl, D_MODEL).astype(BF16)
    cos = cos_ref[...]
    sin = sin_ref[...]
    qk = _dot(hb, w_in_ref[:, OFF_Q:OFF_V])
    for hd in range(H_RET):
        cq = slice(OFF_Q + hd * DK_RET, OFF_Q + (hd + 1) * DK_RET)
        ck = slice(OFF_K + hd * DK_RET, OFF_K + (hd + 1) * DK_RET)
        z_ref[:, cq] = _rope(qk[:, cq], cos, sin)
        z_ref[:, ck] = _rope(qk[:, ck], cos, sin) * (DK_RET ** -0.5)
    for off in range(OFF_V, D_IN, D_MODEL):
        z_ref[:, off:off + D_MODEL] = _dot(hb, w_in_ref[:, off:off + D_MODEL])


def _sample_proj_call(x, ada, g1, cos, sin, w_in):
    B, L, _ = x.shape
    nb = SAMPLE_PROJ_SEQS
    mod = lambda k: pl.BlockSpec((nb, None, 1, D_MODEL), lambda i: (i, k, 0, 0))
    return pl.pallas_call(
        _sample_proj_body,
        grid=(B // nb,),
        in_specs=[pl.BlockSpec((nb, L, D_MODEL), lambda i: (i, 0, 0)), mod(0), mod(1),
                  _const_spec((1, D_MODEL)), _const_spec((nb * L, DK_RET)),
                  _const_spec((nb * L, DK_RET)), _const_spec((D_MODEL, D_IN))],
        out_specs=pl.BlockSpec((nb * L, D_IN), lambda i: (i, 0)),
        out_shape=jax.ShapeDtypeStruct((B * L, D_IN), F32),
        compiler_params=pltpu.CompilerParams(
            dimension_semantics=("arbitrary",), vmem_limit_bytes=VMEM_LIMIT),
        name="sample_proj",
    )(x, ada, ada, g1, cos, sin, w_in)


def _sample_ret_body(cd_ref, qk_ref, v_ref, s_ref, mask_ref, qd_ref, kd_ref, o_ref, sn_ref, *, L):
    nb = s_ref.shape[0]
    for hd in range(H_RET):
        q = qk_ref[:, OFF_Q + hd * DK_RET:OFF_Q + (hd + 1) * DK_RET]
        k = qk_ref[:, OFF_K + hd * DK_RET:OFF_K + (hd + 1) * DK_RET]
        v = v_ref[:, hd * DV_RET:(hd + 1) * DV_RET]
        vb = v.astype(BF16)
        scores = _dot_nt(q.astype(BF16), k.astype(BF16)) * mask_ref[hd]
        o_intra = _dot(scores.astype(BF16), vb)
        qs = q * qd_ref[hd]
        ks = k * kd_ref[hd]
        o_inter = []
        for b in range(nb):
            rows = slice(b * L, (b + 1) * L)
            s_old = s_ref[b, hd]
            o_inter.append(_dot(qs[rows].astype(BF16), s_old.astype(BF16)))
            sn_ref[b, hd] = (s_old * cd_ref[hd]
                             + _dot_tn(ks[rows].astype(BF16), v[rows].astype(BF16)))
        o = o_intra + jnp.concatenate(o_inter, axis=0)
        o_ref[:, hd * DV_RET:(hd + 1) * DV_RET] = _groupnorm(o)


def _sample_ret_call(z, state, layer, tables, L):
    nb = SAMPLE_RET_SEQS
    B = state.shape[1]
    R = nb * L
    mask, qd, kd, cd = tables
    sspec_in = pl.BlockSpec((None, nb, H_RET, DK_RET, DV_RET), lambda i: (layer, i, 0, 0, 0))
    return pl.pallas_call(
        functools.partial(_sample_ret_body, L=L),
        grid=(B // nb,),
        in_specs=[pl.BlockSpec(memory_space=pltpu.SMEM),
                  pl.BlockSpec((R, 2 * D_RET_QK), lambda i: (i, 0)),
                  pl.BlockSpec((R, D_RET_V), lambda i: (i, OFF_V // D_RET_V)),
                  sspec_in,
                  _const_spec(mask.shape), _const_spec(qd.shape), _const_spec(kd.shape)],
        out_specs=[pl.BlockSpec((R, D_RET_V), lambda i: (i, 0)),
                   pl.BlockSpec((nb, H_RET, DK_RET, DV_RET), lambda i: (i, 0, 0, 0))],
        out_shape=[jax.ShapeDtypeStruct((B * L, D_RET_V), F32),
                   jax.ShapeDtypeStruct((B, H_RET, DK_RET, DV_RET), F32)],
        compiler_params=pltpu.CompilerParams(
            dimension_semantics=("arbitrary",), vmem_limit_bytes=VMEM_LIMIT),
        name="sample_ret",
    )(cd, z, z, state, mask, qd, kd)


def _sample_post_body(x_ref, gt_ref, o_ref, g_ref, xl_ref, gl_ref, ma_ref, mb_ref,
                      conv_ref, lru_ref, w_pr_ref, w_pl_ref, w_out_ref,
                      cw_ref, cb_ref, wa_ref, ba_ref, wx_ref, bx_ref, lam_ref,
                      xo_ref, lruo_ref, a_s, u_s):
    x = x_ref[...]
    nb, L, _ = x.shape
    R = nb * L
    p_ret = _dot((o_ref[...] * _silu(g_ref[...])).astype(BF16), w_pr_ref[...])

    xl3 = xl_ref[...].reshape(nb, L, D_LRU)
    buf = conv_ref[...]
    row = lax.broadcasted_iota(jnp.int32, (nb, L, D_LRU), 1)
    xc3 = cw_ref[CONV_W - 1:CONV_W, :] * xl3 + cb_ref[...]
    for k in range(1, CONV_W):
        src = jnp.where(row >= L - k, buf, xl3)
        xc3 = xc3 + cw_ref[CONV_W - 1 - k:CONV_W - k, :] * pltpu.roll(src, k, axis=1)
    xc = xc3.reshape(R, D_LRU)

    ba = ba_ref[...]
    bx = bx_ref[...]
    sp = _softplus(-lam_ref[...])
    hs = []
    for n in range(N_LRU_BLOCKS):
        cols = slice(n * LRU_BLOCK, (n + 1) * LRU_BLOCK)
        a_s[n], u_s[n] = _lru_gate_block(n, xc[:, cols], wa_ref, ba, wx_ref, bx, sp)
        hcur = lru_ref[:, cols]
        for t in range(L):
            idx = pl.ds(t, nb, stride=L)
            hcur = a_s[n, idx, :] * hcur + u_s[n, idx, :]
            u_s[n, idx, :] = hcur
        lruo_ref[:, cols] = hcur
        hs.append(u_s[n])
    y = jnp.concatenate(hs, axis=1) * jax.nn.gelu(gl_ref[...])
    p_lru = _dot(y.astype(BF16), w_pl_ref[...])

    mix = _merge_out(ma_ref[...], mb_ref[...], p_ret, p_lru, w_out_ref)
    xo_ref[...] = x + gt_ref[...] * mix.reshape(nb, L, D_MODEL)


def _sample_post_call(x, ada, o, z, conv_pad, state_lru, layer, w):
    B, L, _ = x.shape
    nb = SAMPLE_POST_SEQS
    R = nb * L
    zcol = lambda off: pl.BlockSpec((R, D_MODEL), lambda i: (i, off // D_MODEL))
    return pl.pallas_call(
        _sample_post_body,
        grid=(B // nb,),
        in_specs=[
            pl.BlockSpec((nb, L, D_MODEL), lambda i: (i, 0, 0)),
            pl.BlockSpec((nb, None, 1, D_MODEL), lambda i: (i, 2, 0, 0)),
            pl.BlockSpec((R, D_RET_V), lambda i: (i, 0)),
            zcol(OFF_G), zcol(OFF_XL), zcol(OFF_GL), zcol(OFF_MA), zcol(OFF_MB),
            pl.BlockSpec((None, nb, SUBLANES, D_LRU), lambda i: (layer, i, 0, 0)),
            pl.BlockSpec((None, nb, D_LRU), lambda i: (layer, i, 0)),
            _const_spec((D_RET_V, D_MODEL)), _const_spec((D_LRU, D_MODEL)),
            _const_spec((D_MODEL, D_MODEL)),
            _const_spec((CONV_W, D_LRU)), _const_spec((1, D_LRU)),
            _const_spec((N_LRU_BLOCKS, LRU_BLOCK, LRU_BLOCK)), _const_spec((1, D_LRU)),
            _const_spec((N_LRU_BLOCKS, LRU_BLOCK, LRU_BLOCK)), _const_spec((1, D_LRU)),
            _const_spec((1, D_LRU)),
        ],
        out_specs=[pl.BlockSpec((nb, L, D_MODEL), lambda i: (i, 0, 0)),
                   pl.BlockSpec((nb, D_LRU), lambda i: (i, 0))],
        out_shape=[jax.ShapeDtypeStruct(x.shape, F32),
                   jax.ShapeDtypeStruct((B, D_LRU), F32)],
        scratch_shapes=[pltpu.VMEM((N_LRU_BLOCKS, R, LRU_BLOCK), F32),
                        pltpu.VMEM((N_LRU_BLOCKS, R, LRU_BLOCK), F32)],
        compiler_params=pltpu.CompilerParams(
            dimension_semantics=("arbitrary",), vmem_limit_bytes=VMEM_LIMIT),
        name="sample_post",
    )(x, ada, o, z, z, z, z, z, conv_pad, state_lru,
      w["w_proj_ret"], w["w_proj_lru"], w["w_out"],
      w["conv_w"], w["conv_b"], w["w_rg_a"], w["b_rg_a"], w["w_rg_x"], w["b_rg_x"],
      w["lru_lambda"])


def _rope_tables(pos):
    half = DK_RET // 2
    inv = ROPE_BASE ** (-jnp.arange(half, dtype=F32) / half)
    ang = pos.astype(F32)[:, None] * inv[None, :]
    cos = jnp.cos(ang)
    sin = jnp.sin(ang)
    return jnp.concatenate([cos, cos], axis=-1), jnp.concatenate([-sin, sin], axis=-1)


def _decay_tables(C):
    log_g = jnp.log1p(-jnp.exp2(-5.0 - jnp.arange(H_RET, dtype=F32)))
    idx = jnp.arange(C, dtype=F32)
    diff = idx[:, None] - idx[None, :]
    intra = jnp.where(diff[None] >= 0,
                      jnp.exp(jnp.maximum(diff, 0.0)[None] * log_g[:, None, None]), 0.0)
    q_decay = jnp.exp((idx + 1.0)[None, :] * log_g[:, None])
    k_decay = jnp.exp((C - 1.0 - idx)[None, :] * log_g[:, None])
    chunk_decay = jnp.exp(C * log_g)
    return intra, q_decay, k_decay, chunk_decay


def _lanes(t, reps=1):
    t = jnp.tile(t, (1, reps))
    return jnp.broadcast_to(t[:, :, None], t.shape + (DK_RET,))


def kernel(x_prompt, x_sample, c_prompt, c_sample, state_ret, state_lru, state_conv, w_ada, b_ada, norm1_g, norm2_g, w_in, w_proj_ret, w_proj_lru, w_out, conv_w, conv_b, w_rg_a, b_rg_a, w_rg_x, b_rg_x, lru_lambda, w_mlp1, w_mlp2, final_g):
    bp, lp, _ = x_prompt.shape
    bs, ls, _ = x_sample.shape
    assert lp % PROMPT_TILE == 0 and PROMPT_TILE % RET_CHUNK == 0
    assert ls == SUBLANES and ls % RET_CHUNK != 0 and ls >= CONV_W - 1

    ada = _ada_call(jnp.concatenate([c_prompt, c_sample], axis=0), w_ada, b_ada)
    ada = ada.reshape(DEPTH, bp + bs, 6, 1, D_MODEL)

    row = lambda p: p.reshape(DEPTH, 1, -1)
    layers = []
    for l in range(DEPTH):
        layers.append(dict(
            w_in=w_in[l].astype(BF16), w_proj_ret=w_proj_ret[l].astype(BF16),
            w_proj_lru=w_proj_lru[l].astype(BF16), w_out=w_out[l].astype(BF16),
            w_rg_a=w_rg_a[l].astype(BF16), w_rg_x=w_rg_x[l].astype(BF16),
            w_mlp1=w_mlp1[l].astype(BF16), w_mlp2=w_mlp2[l].astype(BF16),
            conv_w=conv_w[l], conv_b=row(conv_b)[l], b_rg_a=row(b_rg_a)[l],
            b_rg_x=row(b_rg_x)[l], lru_lambda=row(lru_lambda)[l],
            g1=row(norm1_g)[l], g2=row(norm2_g)[l]))
    gf = final_g.reshape(1, D_MODEL)

    cos_p, sin_p = _rope_tables(jnp.arange(lp, dtype=jnp.int32))
    intra, qd, kd, cd = _decay_tables(RET_CHUNK)
    tables_p = (cos_p, sin_p, intra, _lanes(qd), _lanes(kd), cd)
    x = x_prompt
    ret_p, lru_p, conv_p = [], [], []
    for l, w in enumerate(layers):
        ada_l = ada[l, :bp]
        x, sr, sl, sc = _prompt_mixer_call(x, ada_l, w["g1"], tables_p, w)
        x = _mlp_call(x, ada_l, w["g2"], w["w_mlp1"], w["w_mlp2"], gf, l == DEPTH - 1)
        ret_p.append(sr)
        lru_p.append(sl[:, 0])
        conv_p.append(sc[:, SUBLANES - (CONV_W - 1):])
    y_prompt = x

    cos_s, sin_s = _rope_tables(PAST_LEN + jnp.arange(ls, dtype=jnp.int32))
    cos_s = jnp.tile(cos_s, (SAMPLE_PROJ_SEQS, 1))
    sin_s = jnp.tile(sin_s, (SAMPLE_PROJ_SEQS, 1))
    intra, qd, kd, cd = _decay_tables(ls)
    eye = jnp.eye(SAMPLE_RET_SEQS, dtype=F32)
    mask = jnp.einsum("ab,hqk->haqbk", eye, intra).reshape(
        H_RET, SAMPLE_RET_SEQS * ls, SAMPLE_RET_SEQS * ls)
    tables_s = (mask, _lanes(qd, SAMPLE_RET_SEQS), _lanes(kd, SAMPLE_RET_SEQS), cd)
    conv_pad = jnp.pad(state_conv, ((0, 0), (0, 0), (SUBLANES - (CONV_W - 1), 0), (0, 0)))
    x = x_sample
    ret_s, lru_s, conv_s = [], [], []
    for l, w in enumerate(layers):
        ada_l = ada[l, bp:]
        z = _sample_proj_call(x, ada_l, w["g1"], cos_s, sin_s, w["w_in"])
        o, sr = _sample_ret_call(z, state_ret, l, tables_s, ls)
        x, sl = _sample_post_call(x, ada_l, o, z, conv_pad, state_lru, l, w)
        x = _mlp_call(x, ada_l, w["g2"], w["w_mlp1"], w["w_mlp2"], gf, l == DEPTH - 1)
        ret_s.append(sr)
        lru_s.append(sl)
        conv_s.append(z[:, OFF_XL:OFF_GL].reshape(bs, ls, D_LRU)[:, ls - (CONV_W - 1):])
    y_sample = x

    return (y_prompt, y_sample, jnp.stack(ret_p), jnp.stack(lru_p), jnp.stack(conv_p),
            jnp.stack(ret_s), jnp.stack(lru_s), jnp.stack(conv_s))
```

```python
import functools

import jax
import jax.numpy as jnp
from jax import lax
from jax.experimental import pallas as pl
from jax.experimental.pallas import tpu as pltpu

F32 = jnp.float32
BF16 = jnp.bfloat16

D_MODEL = 1024
DEPTH = 2
PAST_LEN = 16384
H_RET = 4
DK_RET = D_MODEL // 8
DV_RET = D_MODEL // 4
D_RET_QK = H_RET * DK_RET
D_RET_V = H_RET * DV_RET
RET_CHUNK = 128
ROPE_BASE = 10000.0
D_LRU = D_MODEL
N_LRU_BLOCKS = 8
LRU_BLOCK = D_LRU // N_LRU_BLOCKS
CONV_W = 4
RG_C = 8.0
D_FF = 4 * D_MODEL
NORM_EPS = 1e-6
GN_EPS = 1e-5
D_IN = 2 * D_RET_QK + 2 * D_RET_V + 2 * D_LRU + 2 * D_MODEL

OFF_Q = 0
OFF_K = D_RET_QK
OFF_V = 2 * D_RET_QK
OFF_G = OFF_V + D_RET_V
OFF_XL = OFF_G + D_RET_V
OFF_GL = OFF_XL + D_LRU
OFF_MA = OFF_GL + D_LRU
OFF_MB = OFF_MA + D_MODEL

SUBLANES = 8
VMEM_LIMIT = 60 * 1024 * 1024

PROMPT_TILE = 512
PROJ_COLS = 256
MLP_ROWS = 512
SAMPLE_PROJ_SEQS = 32
SAMPLE_RET_SEQS = 8
SAMPLE_POST_SEQS = 32


def _dot(a, b):
    return jnp.dot(a, b, preferred_element_type=F32)


def _dot_nt(a, b):
    return lax.dot_general(a, b, (((1,), (1,)), ((), ())), preferred_element_type=F32)


def _dot_tn(a, b):
    return lax.dot_general(a, b, (((0,), (0,)), ((), ())), preferred_element_type=F32)


def _rmsnorm(x, g):
    ms = jnp.mean(x * x, axis=-1, keepdims=True)
    return (x * lax.rsqrt(ms + NORM_EPS)) * g


def _silu(x):
    return x * jax.nn.sigmoid(x)


def _rope(x, cos, sin_signed):
    return x * cos + pltpu.roll(x, DK_RET // 2, axis=1) * sin_signed


def _groupnorm(o):
    mu = jnp.mean(o, axis=-1, keepdims=True)
    d = o - mu
    var = jnp.mean(d * d, axis=-1, keepdims=True)
    return d * lax.rsqrt(var + GN_EPS)


def _softplus(y):
    return jnp.maximum(y, 0.0) + jnp.log1p(jnp.exp(-jnp.abs(y)))


def _lru_gate_block(n, xc_blk, wax_ref, ba, bx, softplus_neg_lam):
    cols = slice(n * LRU_BLOCK, (n + 1) * LRU_BLOCK)
    rg = _dot(xc_blk.astype(BF16), wax_ref[n])
    r = jax.nn.sigmoid(rg[:, :LRU_BLOCK] + ba[:, cols])
    i = jax.nn.sigmoid(rg[:, LRU_BLOCK:] + bx[:, cols])
    a = jnp.exp((-RG_C) * r * softplus_neg_lam[:, cols])
    return a, jnp.sqrt(1.0 - a * a) * (i * xc_blk)


def _merge_out(ma, mb, p_ret, p_lru, w_out_ref):
    m = jax.nn.sigmoid(ma) * p_ret + jax.nn.sigmoid(mb) * p_lru
    return _dot(m.astype(BF16), w_out_ref[...])


def _ada_body(c_ref, w_ref, b_ref, o_ref):
    s = _silu(c_ref[...]).astype(BF16)
    val = _dot(s, w_ref[...].astype(BF16)) + b_ref[...]
    o_ref[...] = val.reshape(o_ref.shape)


def _ada_call(c_all, w_ada, b_ada):
    n = c_all.shape[0]
    tn = D_MODEL
    return pl.pallas_call(
        _ada_body,
        grid=(DEPTH, 6 * D_MODEL // tn),
        in_specs=[
            pl.BlockSpec((n, D_MODEL), lambda l, j: (0, 0)),
            pl.BlockSpec((None, D_MODEL, tn), lambda l, j: (l, 0, j)),
            pl.BlockSpec((None, 1, tn), lambda l, j: (l, 0, j)),
        ],
        out_specs=pl.BlockSpec((None, n, None, 1, tn), lambda l, j: (l, 0, j, 0, 0)),
        out_shape=jax.ShapeDtypeStruct((DEPTH, n, 6, 1, D_MODEL), F32),
        compiler_params=pltpu.CompilerParams(
            dimension_semantics=("arbitrary", "arbitrary"), vmem_limit_bytes=VMEM_LIMIT),
        name="ada",
    )(c_all, w_ada, b_ada.reshape(DEPTH, 1, 6 * D_MODEL))


def _prompt_mixer_body(cd_ref, x_ref, sh_ref, sc_ref, gt_ref, cos_ref, sin_ref,
                       intra_ref, qd_ref, kd_ref, g1_ref, w_in_ref, w_pr_ref, w_pl_ref, w_out_ref,
                       cw_ref, cb_ref, wax_ref, ba_ref, bx_ref, lam_ref,
                       xo_ref, sret_ref, slru_ref, sconv_ref,
                       hb_s, z_s, xpad_s, a_s, u_s, carry_s, o_s, y_s, pl_s):
    t = pl.program_id(1)
    T = x_ref.shape[1]
    C = RET_CHUNK
    G = T // SUBLANES
    P = G + SUBLANES

    @pl.when(t == 0)
    def _():
        sret_ref[...] = jnp.zeros_like(sret_ref)
        slru_ref[...] = jnp.zeros_like(slru_ref)
        xpad_s[0:SUBLANES, :] = jnp.zeros((SUBLANES, D_LRU), F32)

    x = x_ref[0]
    h = _rmsnorm(x, g1_ref[...]) * (1.0 + sc_ref[0]) + sh_ref[0]
    hb_s[...] = h.astype(BF16)

    def proj(off):
        cols = slice(off, off + PROJ_COLS)
        z_s[:, cols] = _dot(hb_s[...], w_in_ref[:, cols])

    def proj_lru(off):
        cols = slice(off, off + PROJ_COLS)
        pl_s[:, cols] = _dot(y_s[...], w_pl_ref[:, cols])

    for off in range(OFF_XL, OFF_GL, PROJ_COLS):
        proj(off)
    xpad_s[SUBLANES:SUBLANES + T, :] = z_s[:, OFF_XL:OFF_GL]
    sconv_ref[...] = xpad_s[T:T + SUBLANES, :]

    pending = [o for seg in (OFF_Q, OFF_V, OFF_G, OFF_GL) for o in range(seg, seg + D_MODEL, PROJ_COLS)]
    ba = ba_ref[...]
    bx = bx_ref[...]
    sp = _softplus(-lam_ref[...])
    for n in range(N_LRU_BLOCKS):
        cols = slice(n * LRU_BLOCK, (n + 1) * LRU_BLOCK)
        xc = cw_ref[CONV_W - 1:CONV_W, cols] * xpad_s[SUBLANES:SUBLANES + T, cols] + cb_ref[:, cols]
        for k in range(1, CONV_W):
            xc = xc + (cw_ref[CONV_W - 1 - k:CONV_W - k, cols]
                       * xpad_s[SUBLANES - k:SUBLANES - k + T, cols])
        a, u = _lru_gate_block(n, xc, wax_ref, ba, bx, sp)
        for j in range(SUBLANES):
            a_s[n, j * P:j * P + G, :] = a[j * G:(j + 1) * G]
            u_s[n, j * P:j * P + G, :] = u[j * G:(j + 1) * G]
        proj(pending.pop(0))
        hloc = jnp.zeros((SUBLANES, LRU_BLOCK), F32)
        acum = jnp.ones((SUBLANES, LRU_BLOCK), F32)
        for i in range(G):
            idx = pl.ds(i, SUBLANES, stride=P)
            ai = a_s[n, idx, :]
            hloc = ai * hloc + u_s[n, idx, :]
            acum = acum * ai
            u_s[n, idx, :] = hloc
            a_s[n, idx, :] = acum
        carry = slru_ref[:, cols]
        for j in range(SUBLANES):
            carry_s[j:j + 1, cols] = carry
            carry = hloc[j:j + 1, :] + acum[j:j + 1, :] * carry
        slru_ref[:, cols] = carry
        proj(pending.pop(0))
    xpad_s[0:SUBLANES, :] = xpad_s[T:T + SUBLANES, :]
    for n in range(N_LRU_BLOCKS):
        cols = slice(n * LRU_BLOCK, (n + 1) * LRU_BLOCK)
        for j in range(SUBLANES):
            rows = slice(j * G, (j + 1) * G)
            prow = slice(j * P, j * P + G)
            hseq = u_s[n, prow, :] + a_s[n, prow, :] * carry_s[j:j + 1, cols]
            y_s[rows, cols] = (hseq * jax.nn.gelu(z_s[rows, OFF_GL + n * LRU_BLOCK:
                                                      OFF_GL + (n + 1) * LRU_BLOCK])).astype(BF16)

    pending = [(proj, o) for o in range(OFF_MA, D_IN, PROJ_COLS)]
    pending += [(proj_lru, o) for o in range(0, D_MODEL, PROJ_COLS)]
    n_bodies = (T // C) * H_RET
    for c in range(T // C):
        rows = slice(c * C, (c + 1) * C)
        cos = cos_ref[rows, :]
        sin = sin_ref[rows, :]
        for hd in range(H_RET):
            cq = slice(OFF_Q + hd * DK_RET, OFF_Q + (hd + 1) * DK_RET)
            ck = slice(OFF_K + hd * DK_RET, OFF_K + (hd + 1) * DK_RET)
            cv = slice(OFF_V + hd * DV_RET, OFF_V + (hd + 1) * DV_RET)
            qr = _rope(z_s[rows, cq], cos, sin)
            kr = _rope(z_s[rows, ck], cos, sin) * (DK_RET ** -0.5)
            vb = z_s[rows, cv].astype(BF16)
            s_old = sret_ref[hd]
            scores = _dot_nt(qr.astype(BF16), kr.astype(BF16)) * intra_ref[hd]
            o = (_dot(scores.astype(BF16), vb)
                 + _dot((qr * qd_ref[hd]).astype(BF16), s_old.astype(BF16)))
            sret_ref[hd] = (s_old * cd_ref[hd]
                            + _dot_tn((kr * kd_ref[hd]).astype(BF16), vb))
            co = slice(hd * DV_RET, (hd + 1) * DV_RET)
            g = z_s[rows, OFF_G + hd * DV_RET:OFF_G + (hd + 1) * DV_RET]
            o_s[rows, co] = (_groupnorm(o) * _silu(g)).astype(BF16)
            body = c * H_RET + hd
            for fn, off in pending[body * len(pending) // n_bodies:
                                   (body + 1) * len(pending) // n_bodies]:
                fn(off)

    p_ret = _dot(o_s[...], w_pr_ref[...])
    m = (jax.nn.sigmoid(z_s[:, OFF_MA:OFF_MB]) * p_ret
         + jax.nn.sigmoid(z_s[:, OFF_MB:D_IN]) * pl_s[...])
    mix = _dot(m.astype(BF16), w_out_ref[...])
    xo_ref[0] = x + gt_ref[0] * mix


def _const_spec(shape):
    nd = len(shape)
    return pl.BlockSpec(shape, lambda *_: (0,) * nd, pipeline_mode=pl.Buffered(1))


def _layer_spec(arr, layer):
    nd = arr.ndim - 1
    return pl.BlockSpec((None,) + arr.shape[1:], lambda *_: (layer,) + (0,) * nd,
                        pipeline_mode=pl.Buffered(1))


def _mod_spec(layer, row0, nb, k, grid_rank):
    assert row0 % nb == 0
    if grid_rank == 2:
        imap = lambda i, t: (layer, row0 // nb + i, k, 0, 0)
    else:
        imap = lambda i: (layer, row0 // nb + i, k, 0, 0)
    return pl.BlockSpec((None, nb, None, 1, D_MODEL), imap)


_MIXER_PARAMS = ("conv_w", "conv_b", "w_rg", "b_rg_a", "b_rg_x", "lru_lambda")


def _prompt_mixer_call(x, ada, row0, layer, tables, w):
    B, L, _ = x.shape
    T = PROMPT_TILE
    cos, sin, intra, qd, kd, cd = tables
    mod = lambda k: _mod_spec(layer, row0, 1, k, 2)
    weights = [w[k] for k in ("g1", "w_in", "w_proj_ret", "w_proj_lru", "w_out") + _MIXER_PARAMS]
    in_specs = [
        pl.BlockSpec(memory_space=pltpu.SMEM),
        pl.BlockSpec((1, T, D_MODEL), lambda b, t: (b, t, 0)),
        mod(0), mod(1), mod(2),
        pl.BlockSpec((T, DK_RET), lambda b, t: (t, 0)),
        pl.BlockSpec((T, DK_RET), lambda b, t: (t, 0)),
        _const_spec(intra.shape), _const_spec(qd.shape), _const_spec(kd.shape),
    ] + [_layer_spec(a, layer) for a in weights]
    out_specs = [
        pl.BlockSpec((1, T, D_MODEL), lambda b, t: (b, t, 0)),
        pl.BlockSpec((None, H_RET, DK_RET, DV_RET), lambda b, t: (b, 0, 0, 0)),
        pl.BlockSpec((None, 1, D_LRU), lambda b, t: (b, 0, 0)),
        pl.BlockSpec((None, SUBLANES, D_LRU), lambda b, t: (b, 0, 0)),
    ]
    out_shape = [
        jax.ShapeDtypeStruct((B, L, D_MODEL), F32),
        jax.ShapeDtypeStruct((B, H_RET, DK_RET, DV_RET), F32),
        jax.ShapeDtypeStruct((B, 1, D_LRU), F32),
        jax.ShapeDtypeStruct((B, SUBLANES, D_LRU), F32),
    ]
    scan_rows = SUBLANES * (T // SUBLANES + SUBLANES)
    scratch = [
        pltpu.VMEM((T, D_MODEL), BF16),
        pltpu.VMEM((T, D_IN), F32),
        pltpu.VMEM((T + SUBLANES, D_LRU), F32),
        pltpu.VMEM((N_LRU_BLOCKS, scan_rows, LRU_BLOCK), F32),
        pltpu.VMEM((N_LRU_BLOCKS, scan_rows, LRU_BLOCK), F32),
        pltpu.VMEM((SUBLANES, D_LRU), F32),
        pltpu.VMEM((T, D_RET_V), BF16),
        pltpu.VMEM((T, D_LRU), BF16),
        pltpu.VMEM((T, D_MODEL), F32),
    ]
    return pl.pallas_call(
        _prompt_mixer_body,
        grid=(B, L // T),
        in_specs=in_specs, out_specs=out_specs, out_shape=out_shape,
        scratch_shapes=scratch,
        compiler_params=pltpu.CompilerParams(
            dimension_semantics=("arbitrary", "arbitrary"), vmem_limit_bytes=VMEM_LIMIT),
        name="prompt_mixer",
    )(cd, x, ada, ada, ada, cos, sin, intra, qd, kd, *weights)


def _mlp_body(x_ref, sh_ref, sc_ref, gt_ref, g2_ref, w1_ref, w2_ref, gf_ref, o_ref, *, final):
    x = x_ref[...]
    nb, tl, _ = x.shape
    h = _rmsnorm(x, g2_ref[...]) * (1.0 + sc_ref[...]) + sh_ref[...]
    hb = h.reshape(nb * tl, D_MODEL).astype(BF16)
    acc = jnp.zeros((nb * tl, D_MODEL), F32)
    for c in range(D_FF // D_MODEL):
        cols = slice(c * D_MODEL, (c + 1) * D_MODEL)
        f = jnp.maximum(_dot(hb, w1_ref[:, cols]), 0.0)
        acc = acc + _dot((f * f).astype(BF16), w2_ref[cols, :])
    y = x + gt_ref[...] * acc.reshape(nb, tl, D_MODEL)
    if final:
        y = _rmsnorm(y, gf_ref[...])
    o_ref[...] = y


def _mlp_call(x, ada, row0, layer, w, gf, final):
    B, L, _ = x.shape
    if L >= MLP_ROWS:
        nb, tl = 1, MLP_ROWS
    else:
        nb, tl = MLP_ROWS // L, L
    mod = lambda k: _mod_spec(layer, row0, nb, k, 2)
    xspec = pl.BlockSpec((nb, tl, D_MODEL), lambda b, t: (b, t, 0))
    weights = [w["g2"], w["w_mlp1"], w["w_mlp2"]]
    return pl.pallas_call(
        functools.partial(_mlp_body, final=final),
        grid=(B // nb, L // tl),
        in_specs=[xspec, mod(3), mod(4), mod(5)] + [_layer_spec(a, layer) for a in weights]
                 + [_const_spec((1, D_MODEL))],
        out_specs=xspec,
        out_shape=jax.ShapeDtypeStruct(x.shape, F32),
        compiler_params=pltpu.CompilerParams(
            dimension_semantics=("arbitrary", "arbitrary"), vmem_limit_bytes=VMEM_LIMIT),
        name="mlp",
    )(x, ada, ada, ada, *weights, gf)


def _sample_proj_body(x_ref, sh_ref, sc_ref, cos_ref, sin_ref, g1_ref, w_in_ref, z_ref):
    x = x_ref[...]
    nb, tl, _ = x.shape
    h = _rmsnorm(x, g1_ref[...]) * (1.0 + sc_ref[...]) + sh_ref[...]
    hb = h.reshape(nb * tl, D_MODEL).astype(BF16)
    cos = cos_ref[...]
    sin = sin_ref[...]
    qk = _dot(hb, w_in_ref[:, OFF_Q:OFF_V])
    for hd in range(H_RET):
        cq = slice(OFF_Q + hd * DK_RET, OFF_Q + (hd + 1) * DK_RET)
        ck = slice(OFF_K + hd * DK_RET, OFF_K + (hd + 1) * DK_RET)
        z_ref[:, cq] = _rope(qk[:, cq], cos, sin)
        z_ref[:, ck] = _rope(qk[:, ck], cos, sin) * (DK_RET ** -0.5)
    for off in range(OFF_V, D_IN, D_MODEL):
        z_ref[:, off:off + D_MODEL] = _dot(hb, w_in_ref[:, off:off + D_MODEL])


def _sample_proj_call(x, ada, row0, layer, cos, sin, w):
    B, L, _ = x.shape
    nb = SAMPLE_PROJ_SEQS
    mod = lambda k: _mod_spec(layer, row0, nb, k, 1)
    weights = [w["g1"], w["w_in"]]
    return pl.pallas_call(
        _sample_proj_body,
        grid=(B // nb,),
        in_specs=[pl.BlockSpec((nb, L, D_MODEL), lambda i: (i, 0, 0)), mod(0), mod(1),
                  _const_spec((nb * L, DK_RET)), _const_spec((nb * L, DK_RET))]
                 + [_layer_spec(a, layer) for a in weights],
        out_specs=pl.BlockSpec((nb * L, D_IN), lambda i: (i, 0)),
        out_shape=jax.ShapeDtypeStruct((B * L, D_IN), F32),
        compiler_params=pltpu.CompilerParams(
            dimension_semantics=("arbitrary",), vmem_limit_bytes=VMEM_LIMIT),
        name="sample_proj",
    )(x, ada, ada, cos, sin, *weights)


def _sample_ret_body(cd_ref, qk_ref, v_ref, s_ref, mask_ref, qd_ref, kd_ref, *rest, L):
    o_ref, sn_ref = rest[-2:]
    nb = s_ref.shape[0]
    for hd in range(H_RET):
        q = qk_ref[:, OFF_Q + hd * DK_RET:OFF_Q + (hd + 1) * DK_RET]
        k = qk_ref[:, OFF_K + hd * DK_RET:OFF_K + (hd + 1) * DK_RET]
        v = v_ref[:, hd * DV_RET:(hd + 1) * DV_RET]
        vb = v.astype(BF16)
        scores = _dot_nt(q.astype(BF16), k.astype(BF16)) * mask_ref[hd]
        o_intra = _dot(scores.astype(BF16), vb)
        qs = q * qd_ref[hd]
        ks = k * kd_ref[hd]
        o_inter = []
        for b in range(nb):
            rows = slice(b * L, (b + 1) * L)
            s_old = s_ref[b, hd]
            o_inter.append(_dot(qs[rows].astype(BF16), s_old.astype(BF16)))
            sn_ref[b, hd] = (s_old * cd_ref[hd]
                             + _dot_tn(ks[rows].astype(BF16), v[rows].astype(BF16)))
        o = o_intra + jnp.concatenate(o_inter, axis=0)
        o_ref[:, hd * DV_RET:(hd + 1) * DV_RET] = _groupnorm(o)


def _sample_ret_call(z, state, new_state, layer, tables, L):
    nb = SAMPLE_RET_SEQS
    B = state.shape[1]
    R = nb * L
    mask, qd, kd, cd = tables
    sspec = pl.BlockSpec((None, nb, H_RET, DK_RET, DV_RET), lambda i: (layer, i, 0, 0, 0))
    in_specs = [pl.BlockSpec(memory_space=pltpu.SMEM),
                pl.BlockSpec((R, 2 * D_RET_QK), lambda i: (i, 0)),
                pl.BlockSpec((R, D_RET_V), lambda i: (i, OFF_V // D_RET_V)),
                sspec,
                _const_spec(mask.shape), _const_spec(qd.shape), _const_spec(kd.shape)]
    args = [cd, z, z, state, mask, qd, kd]
    aliases = {}
    if new_state is not None:
        in_specs.append(pl.BlockSpec(memory_space=pl.ANY))
        args.append(new_state)
        aliases = {len(args) - 1: 1}
    return pl.pallas_call(
        functools.partial(_sample_ret_body, L=L),
        grid=(B // nb,),
        in_specs=in_specs,
        out_specs=[pl.BlockSpec((R, D_RET_V), lambda i: (i, 0)), sspec],
        out_shape=[jax.ShapeDtypeStruct((B * L, D_RET_V), F32),
                   jax.ShapeDtypeStruct(state.shape, F32)],
        input_output_aliases=aliases,
        compiler_params=pltpu.CompilerParams(
            dimension_semantics=("arbitrary",), vmem_limit_bytes=VMEM_LIMIT),
        name="sample_ret",
    )(*args)


def _sample_post_body(x_ref, gt_ref, o_ref, g_ref, xl_ref, gl_ref, ma_ref, mb_ref,
                      conv_ref, lru_ref, w_pr_ref, w_pl_ref, w_out_ref,
                      cw_ref, cb_ref, wax_ref, ba_ref, bx_ref, lam_ref,
                      xo_ref, lruo_ref, a_s, u_s):
    x = x_ref[...]
    nb, L, _ = x.shape
    R = nb * L
    p_ret = _dot((o_ref[...] * _silu(g_ref[...])).astype(BF16), w_pr_ref[...])

    xl3 = xl_ref[...].reshape(nb, L, D_LRU)
    buf = conv_ref[...]
    row = lax.broadcasted_iota(jnp.int32, (nb, L, D_LRU), 1)
    xc3 = cw_ref[CONV_W - 1:CONV_W, :] * xl3 + cb_ref[...]
    for k in range(1, CONV_W):
        src = jnp.where(row >= L - k, buf, xl3)
        xc3 = xc3 + cw_ref[CONV_W - 1 - k:CONV_W - k, :] * pltpu.roll(src, k, axis=1)
    xc = xc3.reshape(R, D_LRU)

    ba = ba_ref[...]
    bx = bx_ref[...]
    sp = _softplus(-lam_ref[...])
    hs = []
    for n in range(N_LRU_BLOCKS):
        cols = slice(n * LRU_BLOCK, (n + 1) * LRU_BLOCK)
        a_s[n], u_s[n] = _lru_gate_block(n, xc[:, cols], wax_ref, ba, bx, sp)
        hcur = lru_ref[:, cols]
        for t in range(L):
            idx = pl.ds(t, nb, stride=L)
            hcur = a_s[n, idx, :] * hcur + u_s[n, idx, :]
            u_s[n, idx, :] = hcur
        lruo_ref[:, cols] = hcur
        hs.append(u_s[n])
    y = jnp.concatenate(hs, axis=1) * jax.nn.gelu(gl_ref[...])
    p_lru = _dot(y.astype(BF16), w_pl_ref[...])

    mix = _merge_out(ma_ref[...], mb_ref[...], p_ret, p_lru, w_out_ref)
    xo_ref[...] = x + gt_ref[...] * mix.reshape(nb, L, D_MODEL)


def _sample_post_call(x, ada, row0, layer, o, z, conv_pad, state_lru, w):
    B, L, _ = x.shape
    nb = SAMPLE_POST_SEQS
    R = nb * L
    zcol = lambda off: pl.BlockSpec((R, D_MODEL), lambda i: (i, off // D_MODEL))
    weights = [w[k] for k in ("w_proj_ret", "w_proj_lru", "w_out") + _MIXER_PARAMS]
    return pl.pallas_call(
        _sample_post_body,
        grid=(B // nb,),
        in_specs=[
            pl.BlockSpec((nb, L, D_MODEL), lambda i: (i, 0, 0)),
            _mod_spec(layer, row0, nb, 2, 1),
            pl.BlockSpec((R, D_RET_V), lambda i: (i, 0)),
            zcol(OFF_G), zcol(OFF_XL), zcol(OFF_GL), zcol(OFF_MA), zcol(OFF_MB),
            pl.BlockSpec((None, nb, SUBLANES, D_LRU), lambda i: (layer, i, 0, 0)),
            pl.BlockSpec((None, nb, D_LRU), lambda i: (layer, i, 0)),
        ] + [_layer_spec(a, layer) for a in weights],
        out_specs=[pl.BlockSpec((nb, L, D_MODEL), lambda i: (i, 0, 0)),
                   pl.BlockSpec((nb, D_LRU), lambda i: (i, 0))],
        out_shape=[jax.ShapeDtypeStruct(x.shape, F32),
                   jax.ShapeDtypeStruct((B, D_LRU), F32)],
        scratch_shapes=[pltpu.VMEM((N_LRU_BLOCKS, R, LRU_BLOCK), F32),
                        pltpu.VMEM((N_LRU_BLOCKS, R, LRU_BLOCK), F32)],
        compiler_params=pltpu.CompilerParams(
            dimension_semantics=("arbitrary",), vmem_limit_bytes=VMEM_LIMIT),
        name="sample_post",
    )(x, ada, o, z, z, z, z, z, conv_pad, state_lru, *weights)


def _rope_tables(pos):
    half = DK_RET // 2
    inv = ROPE_BASE ** (-jnp.arange(half, dtype=F32) / half)
    ang = pos.astype(F32)[:, None] * inv[None, :]
    cos = jnp.cos(ang)
    sin = jnp.sin(ang)
    return jnp.concatenate([cos, cos], axis=-1), jnp.concatenate([-sin, sin], axis=-1)


def _decay_tables(C):
    log_g = jnp.log1p(-jnp.exp2(-5.0 - jnp.arange(H_RET, dtype=F32)))
    idx = jnp.arange(C, dtype=F32)
    diff = idx[:, None] - idx[None, :]
    intra = jnp.where(diff[None] >= 0,
                      jnp.exp(jnp.maximum(diff, 0.0)[None] * log_g[:, None, None]), 0.0)
    q_decay = jnp.exp((idx + 1.0)[None, :] * log_g[:, None])
    k_decay = jnp.exp((C - 1.0 - idx)[None, :] * log_g[:, None])
    chunk_decay = jnp.exp(C * log_g)
    return intra, q_decay, k_decay, chunk_decay


def _lanes(t, reps=1):
    t = jnp.tile(t, (1, reps))
    return jnp.broadcast_to(t[:, :, None], t.shape + (DK_RET,))


def kernel(x_prompt, x_sample, c_prompt, c_sample, state_ret, state_lru, state_conv, w_ada, b_ada, norm1_g, norm2_g, w_in, w_proj_ret, w_proj_lru, w_out, conv_w, conv_b, w_rg_a, b_rg_a, w_rg_x, b_rg_x, lru_lambda, w_mlp1, w_mlp2, final_g):
    bp, lp, _ = x_prompt.shape
    bs, ls, _ = x_sample.shape
    assert lp % PROMPT_TILE == 0 and PROMPT_TILE % RET_CHUNK == 0
    assert ls == SUBLANES and ls % RET_CHUNK != 0 and ls >= CONV_W - 1

    ada = _ada_call(jnp.concatenate([c_sample, c_prompt], axis=0), w_ada, b_ada)
    row0_s, row0_p = 0, bs

    row = lambda p: p.reshape(DEPTH, 1, -1)
    w = dict(
        w_in=w_in.astype(BF16), w_proj_ret=w_proj_ret.astype(BF16),
        w_proj_lru=w_proj_lru.astype(BF16), w_out=w_out.astype(BF16),
        w_rg=jnp.concatenate([w_rg_a, w_rg_x], axis=-1).astype(BF16),
        w_mlp1=w_mlp1.astype(BF16), w_mlp2=w_mlp2.astype(BF16),
        conv_w=conv_w, conv_b=row(conv_b), b_rg_a=row(b_rg_a), b_rg_x=row(b_rg_x),
        lru_lambda=row(lru_lambda), g1=row(norm1_g), g2=row(norm2_g))
    gf = final_g.reshape(1, D_MODEL)

    cos_p, sin_p = _rope_tables(jnp.arange(lp, dtype=jnp.int32))
    intra, qd, kd, cd = _decay_tables(RET_CHUNK)
    tables_p = (cos_p, sin_p, intra, _lanes(qd), _lanes(kd), cd)
    x = x_prompt
    ret_p, lru_p, conv_p = [], [], []
    for l in range(DEPTH):
        x, sr, sl, sc = _prompt_mixer_call(x, ada, row0_p, l, tables_p, w)
        x = _mlp_call(x, ada, row0_p, l, w, gf, l == DEPTH - 1)
        ret_p.append(sr)
        lru_p.append(sl[:, 0])
        conv_p.append(sc[:, SUBLANES - (CONV_W - 1):])
    y_prompt = x

    cos_s, sin_s = _rope_tables(PAST_LEN + jnp.arange(ls, dtype=jnp.int32))
    cos_s = jnp.tile(cos_s, (SAMPLE_PROJ_SEQS, 1))
    sin_s = jnp.tile(sin_s, (SAMPLE_PROJ_SEQS, 1))
    intra, qd, kd, cd = _decay_tables(ls)
    eye = jnp.eye(SAMPLE_RET_SEQS, dtype=F32)
    mask = jnp.einsum("ab,hqk->haqbk", eye, intra).reshape(
        H_RET, SAMPLE_RET_SEQS * ls, SAMPLE_RET_SEQS * ls)
    tables_s = (mask, _lanes(qd, SAMPLE_RET_SEQS), _lanes(kd, SAMPLE_RET_SEQS), cd)
    conv_pad = jnp.pad(state_conv, ((0, 0), (0, 0), (SUBLANES - (CONV_W - 1), 0), (0, 0)))
    x = x_sample
    ret_s = None
    lru_s, conv_s = [], []
    for l in range(DEPTH):
        z = _sample_proj_call(x, ada, row0_s, l, cos_s, sin_s, w)
        o, ret_s = _sample_ret_call(z, state_ret, ret_s, l, tables_s, ls)
        x, sl = _sample_post_call(x, ada, row0_s, l, o, z, conv_pad, state_lru, w)
        x = _mlp_call(x, ada, row0_s, l, w, gf, l == DEPTH - 1)
        lru_s.append(sl)
        conv_s.append(z[:, OFF_XL:OFF_GL].reshape(bs, ls, D_LRU)[:, ls - (CONV_W - 1):])
    y_sample = x

    return (y_prompt, y_sample, jnp.stack(ret_p), jnp.stack(lru_p), jnp.stack(conv_p),
            ret_s, jnp.stack(lru_s), jnp.stack(conv_s))
```

```python
import functools

import jax
import jax.numpy as jnp
from jax import lax
from jax.experimental import pallas as pl
from jax.experimental.pallas import tpu as pltpu

F32 = jnp.float32
BF16 = jnp.bfloat16

D_MODEL = 1024
DEPTH = 2
PAST_LEN = 16384
H_RET = 4
DK_RET = D_MODEL // 8
DV_RET = D_MODEL // 4
D_RET_QK = H_RET * DK_RET
D_RET_V = H_RET * DV_RET
RET_CHUNK = 128
ROPE_BASE = 10000.0
D_LRU = D_MODEL
N_LRU_BLOCKS = 8
LRU_BLOCK = D_LRU // N_LRU_BLOCKS
CONV_W = 4
RG_C = 8.0
D_FF = 4 * D_MODEL
NORM_EPS = 1e-6
GN_EPS = 1e-5
D_IN = 2 * D_RET_QK + 2 * D_RET_V + 2 * D_LRU + 2 * D_MODEL

OFF_Q = 0
OFF_K = D_RET_QK
OFF_V = 2 * D_RET_QK
OFF_G = OFF_V + D_RET_V
OFF_XL = OFF_G + D_RET_V
OFF_GL = OFF_XL + D_LRU
OFF_MA = OFF_GL + D_LRU
OFF_MB = OFF_MA + D_MODEL

SUBLANES = 8
VMEM_LIMIT = 60 * 1024 * 1024

PROMPT_TILE = 512
PROJ_COLS = 256
MLP_ROWS = 512
SAMPLE_PROJ_SEQS = 32
SAMPLE_RET_SEQS = 8
SAMPLE_POST_SEQS = 32


def _dot(a, b):
    return jnp.dot(a, b, preferred_element_type=F32)


def _dot_nt(a, b):
    return lax.dot_general(a, b, (((1,), (1,)), ((), ())), preferred_element_type=F32)


def _dot_tn(a, b):
    return lax.dot_general(a, b, (((0,), (0,)), ((), ())), preferred_element_type=F32)


def _rmsnorm(x, g):
    ms = jnp.mean(x * x, axis=-1, keepdims=True)
    return (x * lax.rsqrt(ms + NORM_EPS)) * g


def _silu(x):
    return x * jax.nn.sigmoid(x)


def _rope(x, cos, sin_signed):
    return x * cos + pltpu.roll(x, DK_RET // 2, axis=1) * sin_signed


def _groupnorm(o):
    mu = jnp.mean(o, axis=-1, keepdims=True)
    d = o - mu
    var = jnp.mean(d * d, axis=-1, keepdims=True)
    return d * lax.rsqrt(var + GN_EPS)


def _softplus(y):
    return jnp.maximum(y, 0.0) + jnp.log1p(jnp.exp(-jnp.abs(y)))


def _lru_gate_block(n, xc_blk, wax_ref, ba, bx, softplus_neg_lam):
    cols = slice(n * LRU_BLOCK, (n + 1) * LRU_BLOCK)
    rg = _dot(xc_blk.astype(BF16), wax_ref[n])
    r = jax.nn.sigmoid(rg[:, :LRU_BLOCK] + ba[:, cols])
    i = jax.nn.sigmoid(rg[:, LRU_BLOCK:] + bx[:, cols])
    a = jnp.exp((-RG_C) * r * softplus_neg_lam[:, cols])
    return a, jnp.sqrt(1.0 - a * a) * (i * xc_blk)


def _merge_out(ma, mb, p_ret, p_lru, w_out_ref):
    m = jax.nn.sigmoid(ma) * p_ret + jax.nn.sigmoid(mb) * p_lru
    return _dot(m.astype(BF16), w_out_ref[...])


def _ada_body(c_ref, w_ref, b_ref, o_ref):
    s = _silu(c_ref[...]).astype(BF16)
    val = _dot(s, w_ref[...].astype(BF16)) + b_ref[...]
    o_ref[...] = val.reshape(o_ref.shape)


def _ada_call(c_all, w_ada, b_ada):
    n = c_all.shape[0]
    tn = D_MODEL
    return pl.pallas_call(
        _ada_body,
        grid=(DEPTH, 6 * D_MODEL // tn),
        in_specs=[
            pl.BlockSpec((n, D_MODEL), lambda l, j: (0, 0)),
            pl.BlockSpec((None, D_MODEL, tn), lambda l, j: (l, 0, j)),
            pl.BlockSpec((None, 1, tn), lambda l, j: (l, 0, j)),
        ],
        out_specs=pl.BlockSpec((None, n, None, 1, tn), lambda l, j: (l, 0, j, 0, 0)),
        out_shape=jax.ShapeDtypeStruct((DEPTH, n, 6, 1, D_MODEL), F32),
        compiler_params=pltpu.CompilerParams(
            dimension_semantics=("arbitrary", "arbitrary"), vmem_limit_bytes=VMEM_LIMIT),
        name="ada",
    )(c_all, w_ada, b_ada.reshape(DEPTH, 1, 6 * D_MODEL))


def _prompt_mixer_body(cd_ref, x_ref, sh_ref, sc_ref, gt_ref, cos_ref, sin_ref,
                       intra_ref, qd_ref, kd_ref, g1_ref, w_in_ref, w_pr_ref, w_pl_ref, w_out_ref,
                       cw_ref, cb_ref, wax_ref, ba_ref, bx_ref, lam_ref,
                       xo_ref, sret_ref, slru_ref, sconv_ref,
                       hb_s, z_s, xpad_s, a_s, u_s, carry_s, o_s, y_s, pl_s):
    t = pl.program_id(1)
    T = x_ref.shape[1]
    C = RET_CHUNK
    G = T // SUBLANES
    P = G + SUBLANES

    @pl.when(t == 0)
    def _():
        sret_ref[...] = jnp.zeros_like(sret_ref)
        slru_ref[...] = jnp.zeros_like(slru_ref)
        xpad_s[0:SUBLANES, :] = jnp.zeros((SUBLANES, D_LRU), F32)

    x = x_ref[0]
    h = _rmsnorm(x, g1_ref[...]) * (1.0 + sc_ref[0]) + sh_ref[0]
    hb_s[...] = h.astype(BF16)

    def proj(off):
        cols = slice(off, off + PROJ_COLS)
        z_s[:, cols] = _dot(hb_s[...], w_in_ref[:, cols])

    def proj_lru(off):
        cols = slice(off, off + PROJ_COLS)
        pl_s[:, cols] = _dot(y_s[...], w_pl_ref[:, cols])

    for off in range(OFF_XL, OFF_GL, PROJ_COLS):
        proj(off)
    xpad_s[SUBLANES:SUBLANES + T, :] = z_s[:, OFF_XL:OFF_GL]
    sconv_ref[...] = xpad_s[T:T + SUBLANES, :]

    pending = [o for seg in (OFF_Q, OFF_V, OFF_G, OFF_GL) for o in range(seg, seg + D_MODEL, PROJ_COLS)]
    ba = ba_ref[...]
    bx = bx_ref[...]
    sp = _softplus(-lam_ref[...])
    for n in range(N_LRU_BLOCKS):
        cols = slice(n * LRU_BLOCK, (n + 1) * LRU_BLOCK)
        xc = cw_ref[CONV_W - 1:CONV_W, cols] * xpad_s[SUBLANES:SUBLANES + T, cols] + cb_ref[:, cols]
        for k in range(1, CONV_W):
            xc = xc + (cw_ref[CONV_W - 1 - k:CONV_W - k, cols]
                       * xpad_s[SUBLANES - k:SUBLANES - k + T, cols])
        a, u = _lru_gate_block(n, xc, wax_ref, ba, bx, sp)
        for j in range(SUBLANES):
            a_s[n, j * P:j * P + G, :] = a[j * G:(j + 1) * G]
            u_s[n, j * P:j * P + G, :] = u[j * G:(j + 1) * G]
        proj(pending.pop(0))
        hloc = jnp.zeros((SUBLANES, LRU_BLOCK), F32)
        acum = jnp.ones((SUBLANES, LRU_BLOCK), F32)
        for i in range(G):
            idx = pl.ds(i, SUBLANES, stride=P)
            ai = a_s[n, idx, :]
            hloc = ai * hloc + u_s[n, idx, :]
            acum = acum * ai
            u_s[n, idx, :] = hloc
            a_s[n, idx, :] = acum
        carry = slru_ref[:, cols]
        for j in range(SUBLANES):
            carry_s[j:j + 1, cols] = carry
            carry = hloc[j:j + 1, :] + acum[j:j + 1, :] * carry
        slru_ref[:, cols] = carry
        proj(pending.pop(0))
    xpad_s[0:SUBLANES, :] = xpad_s[T:T + SUBLANES, :]
    for n in range(N_LRU_BLOCKS):
        cols = slice(n * LRU_BLOCK, (n + 1) * LRU_BLOCK)
        for j in range(SUBLANES):
            rows = slice(j * G, (j + 1) * G)
            prow = slice(j * P, j * P + G)
            hseq = u_s[n, prow, :] + a_s[n, prow, :] * carry_s[j:j + 1, cols]
            y_s[rows, cols] = (hseq * jax.nn.gelu(z_s[rows, OFF_GL + n * LRU_BLOCK:
                                                      OFF_GL + (n + 1) * LRU_BLOCK])).astype(BF16)

    pending = [(proj, o) for o in range(OFF_MA, D_IN, PROJ_COLS)]
    pending += [(proj_lru, o) for o in range(0, D_MODEL, PROJ_COLS)]
    n_bodies = (T // C) * H_RET
    for c in range(T // C):
        rows = slice(c * C, (c + 1) * C)
        cos = cos_ref[rows, :]
        sin = sin_ref[rows, :]
        for hd in range(H_RET):
            cq = slice(OFF_Q + hd * DK_RET, OFF_Q + (hd + 1) * DK_RET)
            ck = slice(OFF_K + hd * DK_RET, OFF_K + (hd + 1) * DK_RET)
            cv = slice(OFF_V + hd * DV_RET, OFF_V + (hd + 1) * DV_RET)
            qr = _rope(z_s[rows, cq], cos, sin)
            kr = _rope(z_s[rows, ck], cos, sin) * (DK_RET ** -0.5)
            vb = z_s[rows, cv].astype(BF16)
            s_old = sret_ref[hd]
            scores = _dot_nt(qr.astype(BF16), kr.astype(BF16)) * intra_ref[hd]
            o = (_dot(scores.astype(BF16), vb)
                 + _dot((qr * qd_ref[hd]).astype(BF16), s_old.astype(BF16)))
            sret_ref[hd] = (s_old * cd_ref[hd]
                            + _dot_tn((kr * kd_ref[hd]).astype(BF16), vb))
            co = slice(hd * DV_RET, (hd + 1) * DV_RET)
            g = z_s[rows, OFF_G + hd * DV_RET:OFF_G + (hd + 1) * DV_RET]
            o_s[rows, co] = (_groupnorm(o) * _silu(g)).astype(BF16)
            body = c * H_RET + hd
            for fn, off in pending[body * len(pending) // n_bodies:
                                   (body + 1) * len(pending) // n_bodies]:
                fn(off)

    p_ret = _dot(o_s[...], w_pr_ref[...])
    m = (jax.nn.sigmoid(z_s[:, OFF_MA:OFF_MB]) * p_ret
         + jax.nn.sigmoid(z_s[:, OFF_MB:D_IN]) * pl_s[...])
    mix = _dot(m.astype(BF16), w_out_ref[...])
    xo_ref[0] = x + gt_ref[0] * mix


def _const_spec(shape):
    nd = len(shape)
    return pl.BlockSpec(shape, lambda *_: (0,) * nd, pipeline_mode=pl.Buffered(1))


def _layer_spec(arr, layer):
    nd = arr.ndim - 1
    return pl.BlockSpec((None,) + arr.shape[1:], lambda *_: (layer,) + (0,) * nd,
                        pipeline_mode=pl.Buffered(1))


def _mod_spec(layer, row0, nb, k, grid_rank):
    assert row0 % nb == 0
    if grid_rank == 2:
        imap = lambda i, t: (layer, row0 // nb + i, k, 0, 0)
    else:
        imap = lambda i: (layer, row0 // nb + i, k, 0, 0)
    return pl.BlockSpec((None, nb, None, 1, D_MODEL), imap)


_MIXER_PARAMS = ("conv_w", "conv_b", "w_rg", "b_rg_a", "b_rg_x", "lru_lambda")


def _prompt_mixer_call(x, ada, row0, layer, tables, w):
    B, L, _ = x.shape
    T = PROMPT_TILE
    cos, sin, intra, qd, kd, cd = tables
    mod = lambda k: _mod_spec(layer, row0, 1, k, 2)
    weights = [w[k] for k in ("g1", "w_in", "w_proj_ret", "w_proj_lru", "w_out") + _MIXER_PARAMS]
    in_specs = [
        pl.BlockSpec(memory_space=pltpu.SMEM),
        pl.BlockSpec((1, T, D_MODEL), lambda b, t: (b, t, 0)),
        mod(0), mod(1), mod(2),
        pl.BlockSpec((T, DK_RET), lambda b, t: (t, 0)),
        pl.BlockSpec((T, DK_RET), lambda b, t: (t, 0)),
        _const_spec(intra.shape), _const_spec(qd.shape), _const_spec(kd.shape),
    ] + [_layer_spec(a, layer) for a in weights]
    out_specs = [
        pl.BlockSpec((1, T, D_MODEL), lambda b, t: (b, t, 0)),
        pl.BlockSpec((None, H_RET, DK_RET, DV_RET), lambda b, t: (b, 0, 0, 0)),
        pl.BlockSpec((None, 1, D_LRU), lambda b, t: (b, 0, 0)),
        pl.BlockSpec((None, SUBLANES, D_LRU), lambda b, t: (b, 0, 0)),
    ]
    out_shape = [
        jax.ShapeDtypeStruct((B, L, D_MODEL), F32),
        jax.ShapeDtypeStruct((B, H_RET, DK_RET, DV_RET), F32),
        jax.ShapeDtypeStruct((B, 1, D_LRU), F32),
        jax.ShapeDtypeStruct((B, SUBLANES, D_LRU), F32),
    ]
    scan_rows = SUBLANES * (T // SUBLANES + SUBLANES)
    scratch = [
        pltpu.VMEM((T, D_MODEL), BF16),
        pltpu.VMEM((T, D_IN), F32),
        pltpu.VMEM((T + SUBLANES, D_LRU), F32),
        pltpu.VMEM((N_LRU_BLOCKS, scan_rows, LRU_BLOCK), F32),
        pltpu.VMEM((N_LRU_BLOCKS, scan_rows, LRU_BLOCK), F32),
        pltpu.VMEM((SUBLANES, D_LRU), F32),
        pltpu.VMEM((T, D_RET_V), BF16),
        pltpu.VMEM((T, D_LRU), BF16),
        pltpu.VMEM((T, D_MODEL), F32),
    ]
    return pl.pallas_call(
        _prompt_mixer_body,
        grid=(B, L // T),
        in_specs=in_specs, out_specs=out_specs, out_shape=out_shape,
        scratch_shapes=scratch,
        compiler_params=pltpu.CompilerParams(
            dimension_semantics=("arbitrary", "arbitrary"), vmem_limit_bytes=VMEM_LIMIT),
        name="prompt_mixer",
    )(cd, x, ada, ada, ada, cos, sin, intra, qd, kd, *weights)


def _mlp_tile(x_ref, sh_ref, sc_ref, gt_ref, g2_ref, w1_ref, w2_ref, gf_ref, o_ref, final):
    x = x_ref[...]
    nb, tl, _ = x.shape
    h = _rmsnorm(x, g2_ref[...]) * (1.0 + sc_ref[...]) + sh_ref[...]
    hb = h.reshape(nb * tl, D_MODEL).astype(BF16)
    acc = jnp.zeros((nb * tl, D_MODEL), F32)
    for c in range(D_FF // D_MODEL):
        cols = slice(c * D_MODEL, (c + 1) * D_MODEL)
        f = jnp.maximum(_dot(hb, w1_ref[:, cols]), 0.0)
        acc = acc + _dot((f * f).astype(BF16), w2_ref[cols, :])
    y = x + gt_ref[...] * acc.reshape(nb, tl, D_MODEL)
    if final:
        y = _rmsnorm(y, gf_ref[...])
    o_ref[...] = y


def _mlp_body(xp_ref, shp_ref, scp_ref, gtp_ref, xs_ref, shs_ref, scs_ref, gts_ref,
              g2_ref, w1_ref, w2_ref, gf_ref, op_ref, os_ref, *, final, n_prompt):
    i = pl.program_id(0)

    @pl.when(i < n_prompt)
    def _():
        _mlp_tile(xp_ref, shp_ref, scp_ref, gtp_ref, g2_ref, w1_ref, w2_ref, gf_ref, op_ref, final)

    @pl.when(i >= n_prompt)
    def _():
        _mlp_tile(xs_ref, shs_ref, scs_ref, gts_ref, g2_ref, w1_ref, w2_ref, gf_ref, os_ref, final)


def _mlp_call(xp, xs, ada, row0_p, row0_s, layer, w, gf, final):
    bp, lp, _ = xp.shape
    bs, ls, _ = xs.shape
    tp = lp // MLP_ROWS
    nbs = MLP_ROWS // ls
    n_prompt = bp * tp
    n_sample = bs // nbs
    assert lp % MLP_ROWS == 0 and MLP_ROWS % ls == 0 and bs % nbs == 0 and row0_s % nbs == 0
    ip = lambda i: jnp.minimum(i, n_prompt - 1)
    js = lambda i: jnp.maximum(i - n_prompt, 0)
    xp_spec = pl.BlockSpec((1, MLP_ROWS, D_MODEL), lambda i: (ip(i) // tp, ip(i) % tp, 0))
    xs_spec = pl.BlockSpec((nbs, ls, D_MODEL), lambda i: (js(i), 0, 0))
    mod_p = lambda k: pl.BlockSpec((None, 1, None, 1, D_MODEL),
                                   lambda i: (layer, row0_p + ip(i) // tp, k, 0, 0))
    mod_s = lambda k: pl.BlockSpec((None, nbs, None, 1, D_MODEL),
                                   lambda i: (layer, row0_s // nbs + js(i), k, 0, 0))
    weights = [w["g2"], w["w_mlp1"], w["w_mlp2"]]
    return pl.pallas_call(
        functools.partial(_mlp_body, final=final, n_prompt=n_prompt),
        grid=(n_prompt + n_sample,),
        in_specs=[xp_spec, mod_p(3), mod_p(4), mod_p(5), xs_spec, mod_s(3), mod_s(4), mod_s(5)]
                 + [_layer_spec(a, layer) for a in weights] + [_const_spec((1, D_MODEL))],
        out_specs=[xp_spec, xs_spec],
        out_shape=[jax.ShapeDtypeStruct(xp.shape, F32), jax.ShapeDtypeStruct(xs.shape, F32)],
        compiler_params=pltpu.CompilerParams(
            dimension_semantics=("arbitrary",), vmem_limit_bytes=VMEM_LIMIT),
        name="mlp",
    )(xp, ada, ada, ada, xs, ada, ada, ada, *weights, gf)


def _sample_proj_body(x_ref, sh_ref, sc_ref, cos_ref, sin_ref, g1_ref, w_in_ref, z_ref):
    x = x_ref[...]
    nb, tl, _ = x.shape
    h = _rmsnorm(x, g1_ref[...]) * (1.0 + sc_ref[...]) + sh_ref[...]
    hb = h.reshape(nb * tl, D_MODEL).astype(BF16)
    cos = cos_ref[...]
    sin = sin_ref[...]
    qk = _dot(hb, w_in_ref[:, OFF_Q:OFF_V])
    for hd in range(H_RET):
        cq = slice(OFF_Q + hd * DK_RET, OFF_Q + (hd + 1) * DK_RET)
        ck = slice(OFF_K + hd * DK_RET, OFF_K + (hd + 1) * DK_RET)
        z_ref[:, cq] = _rope(qk[:, cq], cos, sin)
        z_ref[:, ck] = _rope(qk[:, ck], cos, sin) * (DK_RET ** -0.5)
    for off in range(OFF_V, D_IN, D_MODEL):
        z_ref[:, off:off + D_MODEL] = _dot(hb, w_in_ref[:, off:off + D_MODEL])


def _sample_proj_call(x, ada, row0, layer, cos, sin, w):
    B, L, _ = x.shape
    nb = SAMPLE_PROJ_SEQS
    mod = lambda k: _mod_spec(layer, row0, nb, k, 1)
    weights = [w["g1"], w["w_in"]]
    return pl.pallas_call(
        _sample_proj_body,
        grid=(B // nb,),
        in_specs=[pl.BlockSpec((nb, L, D_MODEL), lambda i: (i, 0, 0)), mod(0), mod(1),
                  _const_spec((nb * L, DK_RET)), _const_spec((nb * L, DK_RET))]
                 + [_layer_spec(a, layer) for a in weights],
        out_specs=pl.BlockSpec((nb * L, D_IN), lambda i: (i, 0)),
        out_shape=jax.ShapeDtypeStruct((B * L, D_IN), F32),
        compiler_params=pltpu.CompilerParams(
            dimension_semantics=("arbitrary",), vmem_limit_bytes=VMEM_LIMIT),
        name="sample_proj",
    )(x, ada, ada, cos, sin, *weights)


def _sample_ret_body(cd_ref, qk_ref, v_ref, s_ref, mask_ref, qd_ref, kd_ref, *rest, L):
    o_ref, sn_ref = rest[-2:]
    nb = s_ref.shape[0]
    for hd in range(H_RET):
        q = qk_ref[:, OFF_Q + hd * DK_RET:OFF_Q + (hd + 1) * DK_RET]
        k = qk_ref[:, OFF_K + hd * DK_RET:OFF_K + (hd + 1) * DK_RET]
        v = v_ref[:, hd * DV_RET:(hd + 1) * DV_RET]
        vb = v.astype(BF16)
        scores = _dot_nt(q.astype(BF16), k.astype(BF16)) * mask_ref[hd]
        o_intra = _dot(scores.astype(BF16), vb)
        qs = q * qd_ref[hd]
        ks = k * kd_ref[hd]
        o_inter = []
        for b in range(nb):
            rows = slice(b * L, (b + 1) * L)
            s_old = s_ref[b, hd]
            o_inter.append(_dot(qs[rows].astype(BF16), s_old.astype(BF16)))
            sn_ref[b, hd] = (s_old * cd_ref[hd]
                             + _dot_tn(ks[rows].astype(BF16), v[rows].astype(BF16)))
        o = o_intra + jnp.concatenate(o_inter, axis=0)
        o_ref[:, hd * DV_RET:(hd + 1) * DV_RET] = _groupnorm(o)


def _sample_ret_call(z, state, new_state, layer, tables, L):
    nb = SAMPLE_RET_SEQS
    B = state.shape[1]
    R = nb * L
    mask, qd, kd, cd = tables
    sspec = pl.BlockSpec((None, nb, H_RET, DK_RET, DV_RET), lambda i: (layer, i, 0, 0, 0))
    in_specs = [pl.BlockSpec(memory_space=pltpu.SMEM),
                pl.BlockSpec((R, 2 * D_RET_QK), lambda i: (i, 0)),
                pl.BlockSpec((R, D_RET_V), lambda i: (i, OFF_V // D_RET_V)),
                sspec,
                _const_spec(mask.shape), _const_spec(qd.shape), _const_spec(kd.shape)]
    args = [cd, z, z, state, mask, qd, kd]
    aliases = {}
    if new_state is not None:
        in_specs.append(pl.BlockSpec(memory_space=pl.ANY))
        args.append(new_state)
        aliases = {len(args) - 1: 1}
    return pl.pallas_call(
        functools.partial(_sample_ret_body, L=L),
        grid=(B // nb,),
        in_specs=in_specs,
        out_specs=[pl.BlockSpec((R, D_RET_V), lambda i: (i, 0)), sspec],
        out_shape=[jax.ShapeDtypeStruct((B * L, D_RET_V), F32),
                   jax.ShapeDtypeStruct(state.shape, F32)],
        input_output_aliases=aliases,
        compiler_params=pltpu.CompilerParams(
            dimension_semantics=("arbitrary",), vmem_limit_bytes=VMEM_LIMIT),
        name="sample_ret",
    )(*args)


def _sample_post_body(x_ref, gt_ref, o_ref, g_ref, xl_ref, gl_ref, ma_ref, mb_ref,
                      conv_ref, lru_ref, w_pr_ref, w_pl_ref, w_out_ref,
                      cw_ref, cb_ref, wax_ref, ba_ref, bx_ref, lam_ref,
                      xo_ref, lruo_ref, a_s, u_s):
    x = x_ref[...]
    nb, L, _ = x.shape
    R = nb * L
    p_ret = _dot((o_ref[...] * _silu(g_ref[...])).astype(BF16), w_pr_ref[...])

    xl3 = xl_ref[...].reshape(nb, L, D_LRU)
    buf = conv_ref[...]
    row = lax.broadcasted_iota(jnp.int32, (nb, L, D_LRU), 1)
    xc3 = cw_ref[CONV_W - 1:CONV_W, :] * xl3 + cb_ref[...]
    for k in range(1, CONV_W):
        src = jnp.where(row >= L - k, buf, xl3)
        xc3 = xc3 + cw_ref[CONV_W - 1 - k:CONV_W - k, :] * pltpu.roll(src, k, axis=1)
    xc = xc3.reshape(R, D_LRU)

    ba = ba_ref[...]
    bx = bx_ref[...]
    sp = _softplus(-lam_ref[...])
    hs = []
    for n in range(N_LRU_BLOCKS):
        cols = slice(n * LRU_BLOCK, (n + 1) * LRU_BLOCK)
        a_s[n], u_s[n] = _lru_gate_block(n, xc[:, cols], wax_ref, ba, bx, sp)
        hcur = lru_ref[:, cols]
        for t in range(L):
            idx = pl.ds(t, nb, stride=L)
            hcur = a_s[n, idx, :] * hcur + u_s[n, idx, :]
            u_s[n, idx, :] = hcur
        lruo_ref[:, cols] = hcur
        hs.append(u_s[n])
    y = jnp.concatenate(hs, axis=1) * jax.nn.gelu(gl_ref[...])
    p_lru = _dot(y.astype(BF16), w_pl_ref[...])

    mix = _merge_out(ma_ref[...], mb_ref[...], p_ret, p_lru, w_out_ref)
    xo_ref[...] = x + gt_ref[...] * mix.reshape(nb, L, D_MODEL)


def _sample_post_call(x, ada, row0, layer, o, z, conv_pad, state_lru, w):
    B, L, _ = x.shape
    nb = SAMPLE_POST_SEQS
    R = nb * L
    zcol = lambda off: pl.BlockSpec((R, D_MODEL), lambda i: (i, off // D_MODEL))
    weights = [w[k] for k in ("w_proj_ret", "w_proj_lru", "w_out") + _MIXER_PARAMS]
    return pl.pallas_call(
        _sample_post_body,
        grid=(B // nb,),
        in_specs=[
            pl.BlockSpec((nb, L, D_MODEL), lambda i: (i, 0, 0)),
            _mod_spec(layer, row0, nb, 2, 1),
            pl.BlockSpec((R, D_RET_V), lambda i: (i, 0)),
            zcol(OFF_G), zcol(OFF_XL), zcol(OFF_GL), zcol(OFF_MA), zcol(OFF_MB),
            pl.BlockSpec((None, nb, SUBLANES, D_LRU), lambda i: (layer, i, 0, 0)),
            pl.BlockSpec((None, nb, D_LRU), lambda i: (layer, i, 0)),
        ] + [_layer_spec(a, layer) for a in weights],
        out_specs=[pl.BlockSpec((nb, L, D_MODEL), lambda i: (i, 0, 0)),
                   pl.BlockSpec((nb, D_LRU), lambda i: (i, 0))],
        out_shape=[jax.ShapeDtypeStruct(x.shape, F32),
                   jax.ShapeDtypeStruct((B, D_LRU), F32)],
        scratch_shapes=[pltpu.VMEM((N_LRU_BLOCKS, R, LRU_BLOCK), F32),
                        pltpu.VMEM((N_LRU_BLOCKS, R, LRU_BLOCK), F32)],
        compiler_params=pltpu.CompilerParams(
            dimension_semantics=("arbitrary",), vmem_limit_bytes=VMEM_LIMIT),
        name="sample_post",
    )(x, ada, o, z, z, z, z, z, conv_pad, state_lru, *weights)


def _rope_tables(pos):
    half = DK_RET // 2
    inv = ROPE_BASE ** (-jnp.arange(half, dtype=F32) / half)
    ang = pos.astype(F32)[:, None] * inv[None, :]
    cos = jnp.cos(ang)
    sin = jnp.sin(ang)
    return jnp.concatenate([cos, cos], axis=-1), jnp.concatenate([-sin, sin], axis=-1)


def _decay_tables(C):
    log_g = jnp.log1p(-jnp.exp2(-5.0 - jnp.arange(H_RET, dtype=F32)))
    idx = jnp.arange(C, dtype=F32)
    diff = idx[:, None] - idx[None, :]
    intra = jnp.where(diff[None] >= 0,
                      jnp.exp(jnp.maximum(diff, 0.0)[None] * log_g[:, None, None]), 0.0)
    q_decay = jnp.exp((idx + 1.0)[None, :] * log_g[:, None])
    k_decay = jnp.exp((C - 1.0 - idx)[None, :] * log_g[:, None])
    chunk_decay = jnp.exp(C * log_g)
    return intra, q_decay, k_decay, chunk_decay


def _lanes(t, reps=1):
    t = jnp.tile(t, (1, reps))
    return jnp.broadcast_to(t[:, :, None], t.shape + (DK_RET,))


def kernel(x_prompt, x_sample, c_prompt, c_sample, state_ret, state_lru, state_conv, w_ada, b_ada, norm1_g, norm2_g, w_in, w_proj_ret, w_proj_lru, w_out, conv_w, conv_b, w_rg_a, b_rg_a, w_rg_x, b_rg_x, lru_lambda, w_mlp1, w_mlp2, final_g):
    bp, lp, _ = x_prompt.shape
    bs, ls, _ = x_sample.shape
    assert lp % PROMPT_TILE == 0 and PROMPT_TILE % RET_CHUNK == 0
    assert ls == SUBLANES and ls % RET_CHUNK != 0 and ls >= CONV_W - 1

    ada = _ada_call(jnp.concatenate([c_sample, c_prompt], axis=0), w_ada, b_ada)
    row0_s, row0_p = 0, bs

    row = lambda p: p.reshape(DEPTH, 1, -1)
    w = dict(
        w_in=w_in.astype(BF16), w_proj_ret=w_proj_ret.astype(BF16),
        w_proj_lru=w_proj_lru.astype(BF16), w_out=w_out.astype(BF16),
        w_rg=jnp.concatenate([w_rg_a, w_rg_x], axis=-1).astype(BF16),
        w_mlp1=w_mlp1.astype(BF16), w_mlp2=w_mlp2.astype(BF16),
        conv_w=conv_w, conv_b=row(conv_b), b_rg_a=row(b_rg_a), b_rg_x=row(b_rg_x),
        lru_lambda=row(lru_lambda), g1=row(norm1_g), g2=row(norm2_g))
    gf = final_g.reshape(1, D_MODEL)

    cos_p, sin_p = _rope_tables(jnp.arange(lp, dtype=jnp.int32))
    intra, qd, kd, cd = _decay_tables(RET_CHUNK)
    tables_p = (cos_p, sin_p, intra, _lanes(qd), _lanes(kd), cd)

    cos_s, sin_s = _rope_tables(PAST_LEN + jnp.arange(ls, dtype=jnp.int32))
    cos_s = jnp.tile(cos_s, (SAMPLE_PROJ_SEQS, 1))
    sin_s = jnp.tile(sin_s, (SAMPLE_PROJ_SEQS, 1))
    intra, qd, kd, cd = _decay_tables(ls)
    eye = jnp.eye(SAMPLE_RET_SEQS, dtype=F32)
    mask = jnp.einsum("ab,hqk->haqbk", eye, intra).reshape(
        H_RET, SAMPLE_RET_SEQS * ls, SAMPLE_RET_SEQS * ls)
    tables_s = (mask, _lanes(qd, SAMPLE_RET_SEQS), _lanes(kd, SAMPLE_RET_SEQS), cd)
    conv_pad = jnp.pad(state_conv, ((0, 0), (0, 0), (SUBLANES - (CONV_W - 1), 0), (0, 0)))

    xp, xs = x_prompt, x_sample
    ret_p, lru_p, conv_p = [], [], []
    ret_s = None
    lru_s, conv_s = [], []
    for l in range(DEPTH):
        xp, sr, sl, sc = _prompt_mixer_call(xp, ada, row0_p, l, tables_p, w)
        ret_p.append(sr)
        lru_p.append(sl[:, 0])
        conv_p.append(sc[:, SUBLANES - (CONV_W - 1):])
        z = _sample_proj_call(xs, ada, row0_s, l, cos_s, sin_s, w)
        o, ret_s = _sample_ret_call(z, state_ret, ret_s, l, tables_s, ls)
        xs, sl = _sample_post_call(xs, ada, row0_s, l, o, z, conv_pad, state_lru, w)
        lru_s.append(sl)
        conv_s.append(z[:, OFF_XL:OFF_GL].reshape(bs, ls, D_LRU)[:, ls - (CONV_W - 1):])
        xp, xs = _mlp_call(xp, xs, ada, row0_p, row0_s, l, w, gf, l == DEPTH - 1)
    y_prompt, y_sample = xp, xs

    return (y_prompt, y_sample, jnp.stack(ret_p), jnp.stack(lru_p), jnp.stack(conv_p),
            ret_s, jnp.stack(lru_s), jnp.stack(conv_s))
```

```python
import functools

import jax
import jax.numpy as jnp
from jax import lax
from jax.experimental import pallas as pl
from jax.experimental.pallas import tpu as pltpu

F32 = jnp.float32
BF16 = jnp.bfloat16

D_MODEL = 1024
DEPTH = 2
PAST_LEN = 16384
H_RET = 4
DK_RET = D_MODEL // 8
DV_RET = D_MODEL // 4
D_RET_QK = H_RET * DK_RET
D_RET_V = H_RET * DV_RET
RET_CHUNK = 128
ROPE_BASE = 10000.0
D_LRU = D_MODEL
N_LRU_BLOCKS = 8
LRU_BLOCK = D_LRU // N_LRU_BLOCKS
CONV_W = 4
RG_C = 8.0
D_FF = 4 * D_MODEL
NORM_EPS = 1e-6
GN_EPS = 1e-5
D_IN = 2 * D_RET_QK + 2 * D_RET_V + 2 * D_LRU + 2 * D_MODEL

OFF_Q = 0
OFF_K = D_RET_QK
OFF_V = 2 * D_RET_QK
OFF_G = OFF_V + D_RET_V
OFF_XL = OFF_G + D_RET_V
OFF_GL = OFF_XL + D_LRU
OFF_MA = OFF_GL + D_LRU
OFF_MB = OFF_MA + D_MODEL

SUBLANES = 8
VMEM_LIMIT = 60 * 1024 * 1024

PROMPT_TILE = 512
PROJ_COLS = 256
MLP_ROWS = 512
SAMPLE_PROJ_SEQS = 32
SAMPLE_RET_SEQS = 8
SAMPLE_POST_SEQS = 32


def _dot(a, b):
    return jnp.dot(a, b, preferred_element_type=F32)


def _dot_nt(a, b):
    return lax.dot_general(a, b, (((1,), (1,)), ((), ())), preferred_element_type=F32)


def _dot_tn(a, b):
    return lax.dot_general(a, b, (((0,), (0,)), ((), ())), preferred_element_type=F32)


def _rmsnorm(x, g):
    ms = jnp.mean(x * x, axis=-1, keepdims=True)
    return (x * lax.rsqrt(ms + NORM_EPS)) * g


def _silu(x):
    return x * jax.nn.sigmoid(x)


def _rope(x, cos, sin_signed):
    return x * cos + pltpu.roll(x, DK_RET // 2, axis=1) * sin_signed


def _groupnorm(o):
    mu = jnp.mean(o, axis=-1, keepdims=True)
    d = o - mu
    var = jnp.mean(d * d, axis=-1, keepdims=True)
    return d * lax.rsqrt(var + GN_EPS)


def _softplus(y):
    return jnp.maximum(y, 0.0) + jnp.log1p(jnp.exp(-jnp.abs(y)))


def _lru_gate_block(n, xc_blk, wax_ref, ba, bx, softplus_neg_lam):
    cols = slice(n * LRU_BLOCK, (n + 1) * LRU_BLOCK)
    rg = _dot(xc_blk.astype(BF16), wax_ref[n])
    r = jax.nn.sigmoid(rg[:, :LRU_BLOCK] + ba[:, cols])
    i = jax.nn.sigmoid(rg[:, LRU_BLOCK:] + bx[:, cols])
    a = jnp.exp((-RG_C) * r * softplus_neg_lam[:, cols])
    return a, jnp.sqrt(1.0 - a * a) * (i * xc_blk)


def _merge_out(ma, mb, p_ret, p_lru, w_out_ref):
    m = jax.nn.sigmoid(ma) * p_ret + jax.nn.sigmoid(mb) * p_lru
    return _dot(m.astype(BF16), w_out_ref[...])


def _ada_body(c_ref, w_ref, b_ref, o_ref):
    s = _silu(c_ref[...]).astype(BF16)
    val = _dot(s, w_ref[...].astype(BF16)) + b_ref[...]
    o_ref[...] = val.reshape(o_ref.shape)


def _ada_call(c_all, w_ada, b_ada):
    n = c_all.shape[0]
    tn = D_MODEL
    return pl.pallas_call(
        _ada_body,
        grid=(DEPTH, 6 * D_MODEL // tn),
        in_specs=[
            pl.BlockSpec((n, D_MODEL), lambda l, j: (0, 0)),
            pl.BlockSpec((None, D_MODEL, tn), lambda l, j: (l, 0, j)),
            pl.BlockSpec((None, 1, tn), lambda l, j: (l, 0, j)),
        ],
        out_specs=pl.BlockSpec((None, n, None, 1, tn), lambda l, j: (l, 0, j, 0, 0)),
        out_shape=jax.ShapeDtypeStruct((DEPTH, n, 6, 1, D_MODEL), F32),
        compiler_params=pltpu.CompilerParams(
            dimension_semantics=("arbitrary", "arbitrary"), vmem_limit_bytes=VMEM_LIMIT),
        name="ada",
    )(c_all, w_ada, b_ada.reshape(DEPTH, 1, 6 * D_MODEL))


def _prompt_mixer_body(cd_ref, x_ref, sh_ref, sc_ref, gt_ref, cos_ref, sin_ref,
                       intra_ref, qd_ref, kd_ref, g1_ref, w_in_ref, w_pr_ref, w_pl_ref, w_out_ref,
                       cw_ref, cb_ref, wax_ref, ba_ref, bx_ref, lam_ref,
                       xo_ref, sret_ref, slru_ref, sconv_ref,
                       hb_s, z_s, xpad_s, a_s, u_s, carry_s, o_s, y_s, pl_s):
    t = pl.program_id(1)
    T = x_ref.shape[1]
    C = RET_CHUNK
    G = T // SUBLANES
    P = G + SUBLANES

    @pl.when(t == 0)
    def _():
        sret_ref[...] = jnp.zeros_like(sret_ref)
        slru_ref[...] = jnp.zeros_like(slru_ref)
        xpad_s[0:SUBLANES, :] = jnp.zeros((SUBLANES, D_LRU), F32)

    x = x_ref[0]
    h = _rmsnorm(x, g1_ref[...]) * (1.0 + sc_ref[0]) + sh_ref[0]
    hb_s[...] = h.astype(BF16)

    def proj(off):
        cols = slice(off, off + PROJ_COLS)
        z_s[:, cols] = _dot(hb_s[...], w_in_ref[:, cols])

    def proj_lru(off):
        cols = slice(off, off + PROJ_COLS)
        pl_s[:, cols] = _dot(y_s[...], w_pl_ref[:, cols])

    for off in range(OFF_XL, OFF_GL, PROJ_COLS):
        proj(off)
    xpad_s[SUBLANES:SUBLANES + T, :] = z_s[:, OFF_XL:OFF_GL]
    sconv_ref[...] = xpad_s[T:T + SUBLANES, :]

    pending = [o for seg in (OFF_Q, OFF_V, OFF_G, OFF_GL) for o in range(seg, seg + D_MODEL, PROJ_COLS)]
    ba = ba_ref[...]
    bx = bx_ref[...]
    sp = _softplus(-lam_ref[...])
    for n in range(N_LRU_BLOCKS):
        cols = slice(n * LRU_BLOCK, (n + 1) * LRU_BLOCK)
        xc = cw_ref[CONV_W - 1:CONV_W, cols] * xpad_s[SUBLANES:SUBLANES + T, cols] + cb_ref[:, cols]
        for k in range(1, CONV_W):
            xc = xc + (cw_ref[CONV_W - 1 - k:CONV_W - k, cols]
                       * xpad_s[SUBLANES - k:SUBLANES - k + T, cols])
        a, u = _lru_gate_block(n, xc, wax_ref, ba, bx, sp)
        for j in range(SUBLANES):
            a_s[n, j * P:j * P + G, :] = a[j * G:(j + 1) * G]
            u_s[n, j * P:j * P + G, :] = u[j * G:(j + 1) * G]
        proj(pending.pop(0))
        hloc = jnp.zeros((SUBLANES, LRU_BLOCK), F32)
        acum = jnp.ones((SUBLANES, LRU_BLOCK), F32)
        for i in range(G):
            idx = pl.ds(i, SUBLANES, stride=P)
            ai = a_s[n, idx, :]
            hloc = ai * hloc + u_s[n, idx, :]
            acum = acum * ai
            u_s[n, idx, :] = hloc
            a_s[n, idx, :] = acum
        carry = slru_ref[:, cols]
        for j in range(SUBLANES):
            carry_s[j:j + 1, cols] = carry
            carry = hloc[j:j + 1, :] + acum[j:j + 1, :] * carry
        slru_ref[:, cols] = carry
        proj(pending.pop(0))
    xpad_s[0:SUBLANES, :] = xpad_s[T:T + SUBLANES, :]
    for n in range(N_LRU_BLOCKS):
        cols = slice(n * LRU_BLOCK, (n + 1) * LRU_BLOCK)
        for j in range(SUBLANES):
            rows = slice(j * G, (j + 1) * G)
            prow = slice(j * P, j * P + G)
            hseq = u_s[n, prow, :] + a_s[n, prow, :] * carry_s[j:j + 1, cols]
            y_s[rows, cols] = (hseq * jax.nn.gelu(z_s[rows, OFF_GL + n * LRU_BLOCK:
                                                      OFF_GL + (n + 1) * LRU_BLOCK])).astype(BF16)

    pending = [(proj, o) for o in range(OFF_MA, D_IN, PROJ_COLS)]
    pending += [(proj_lru, o) for o in range(0, D_MODEL, PROJ_COLS)]
    n_bodies = (T // C) * H_RET
    for c in range(T // C):
        rows = slice(c * C, (c + 1) * C)
        cos = cos_ref[rows, :]
        sin = sin_ref[rows, :]
        for hd in range(H_RET):
            cq = slice(OFF_Q + hd * DK_RET, OFF_Q + (hd + 1) * DK_RET)
            ck = slice(OFF_K + hd * DK_RET, OFF_K + (hd + 1) * DK_RET)
            cv = slice(OFF_V + hd * DV_RET, OFF_V + (hd + 1) * DV_RET)
            qr = _rope(z_s[rows, cq], cos, sin)
            kr = _rope(z_s[rows, ck], cos, sin) * (DK_RET ** -0.5)
            vb = z_s[rows, cv].astype(BF16)
            s_old = sret_ref[hd]
            scores = _dot_nt(qr.astype(BF16), kr.astype(BF16)) * intra_ref[hd]
            o = (_dot(scores.astype(BF16), vb)
                 + _dot((qr * qd_ref[hd]).astype(BF16), s_old.astype(BF16)))
            sret_ref[hd] = (s_old * cd_ref[hd]
                            + _dot_tn((kr * kd_ref[hd]).astype(BF16), vb))
            co = slice(hd * DV_RET, (hd + 1) * DV_RET)
            g = z_s[rows, OFF_G + hd * DV_RET:OFF_G + (hd + 1) * DV_RET]
            o_s[rows, co] = (_groupnorm(o) * _silu(g)).astype(BF16)
            body = c * H_RET + hd
            for fn, off in pending[body * len(pending) // n_bodies:
                                   (body + 1) * len(pending) // n_bodies]:
                fn(off)

    p_ret = _dot(o_s[...], w_pr_ref[...])
    m = (jax.nn.sigmoid(z_s[:, OFF_MA:OFF_MB]) * p_ret
         + jax.nn.sigmoid(z_s[:, OFF_MB:D_IN]) * pl_s[...])
    mix = _dot(m.astype(BF16), w_out_ref[...])
    xo_ref[0] = x + gt_ref[0] * mix


def _const_spec(shape):
    nd = len(shape)
    return pl.BlockSpec(shape, lambda *_: (0,) * nd, pipeline_mode=pl.Buffered(1))


def _layer_spec(arr, layer):
    nd = arr.ndim - 1
    return pl.BlockSpec((None,) + arr.shape[1:], lambda *_: (layer,) + (0,) * nd,
                        pipeline_mode=pl.Buffered(1))


def _mod_spec(layer, row0, nb, k, grid_rank):
    assert row0 % nb == 0
    if grid_rank == 2:
        imap = lambda i, t: (layer, row0 // nb + i, k, 0, 0)
    else:
        imap = lambda i: (layer, row0 // nb + i, k, 0, 0)
    return pl.BlockSpec((None, nb, None, 1, D_MODEL), imap)


_MIXER_PARAMS = ("conv_w", "conv_b", "w_rg", "b_rg_a", "b_rg_x", "lru_lambda")


def _prompt_mixer_call(x, ada, row0, layer, tables, w):
    B, L, _ = x.shape
    T = PROMPT_TILE
    cos, sin, intra, qd, kd, cd = tables
    mod = lambda k: _mod_spec(layer, row0, 1, k, 2)
    weights = [w[k] for k in ("g1", "w_in", "w_proj_ret", "w_proj_lru", "w_out") + _MIXER_PARAMS]
    in_specs = [
        pl.BlockSpec(memory_space=pltpu.SMEM),
        pl.BlockSpec((1, T, D_MODEL), lambda b, t: (b, t, 0)),
        mod(0), mod(1), mod(2),
        pl.BlockSpec((T, DK_RET), lambda b, t: (t, 0)),
        pl.BlockSpec((T, DK_RET), lambda b, t: (t, 0)),
        _const_spec(intra.shape), _const_spec(qd.shape), _const_spec(kd.shape),
    ] + [_layer_spec(a, layer) for a in weights]
    out_specs = [
        pl.BlockSpec((1, T, D_MODEL), lambda b, t: (b, t, 0)),
        pl.BlockSpec((None, H_RET, DK_RET, DV_RET), lambda b, t: (b, 0, 0, 0)),
        pl.BlockSpec((None, 1, D_LRU), lambda b, t: (b, 0, 0)),
        pl.BlockSpec((None, SUBLANES, D_LRU), lambda b, t: (b, 0, 0)),
    ]
    out_shape = [
        jax.ShapeDtypeStruct((B, L, D_MODEL), F32),
        jax.ShapeDtypeStruct((B, H_RET, DK_RET, DV_RET), F32),
        jax.ShapeDtypeStruct((B, 1, D_LRU), F32),
        jax.ShapeDtypeStruct((B, SUBLANES, D_LRU), F32),
    ]
    scan_rows = SUBLANES * (T // SUBLANES + SUBLANES)
    scratch = [
        pltpu.VMEM((T, D_MODEL), BF16),
        pltpu.VMEM((T, D_IN), F32),
        pltpu.VMEM((T + SUBLANES, D_LRU), F32),
        pltpu.VMEM((N_LRU_BLOCKS, scan_rows, LRU_BLOCK), F32),
        pltpu.VMEM((N_LRU_BLOCKS, scan_rows, LRU_BLOCK), F32),
        pltpu.VMEM((SUBLANES, D_LRU), F32),
        pltpu.VMEM((T, D_RET_V), BF16),
        pltpu.VMEM((T, D_LRU), BF16),
        pltpu.VMEM((T, D_MODEL), F32),
    ]
    return pl.pallas_call(
        _prompt_mixer_body,
        grid=(B, L // T),
        in_specs=in_specs, out_specs=out_specs, out_shape=out_shape,
        scratch_shapes=scratch,
        compiler_params=pltpu.CompilerParams(
            dimension_semantics=("arbitrary", "arbitrary"), vmem_limit_bytes=VMEM_LIMIT),
        name="prompt_mixer",
    )(cd, x, ada, ada, ada, cos, sin, intra, qd, kd, *weights)


def _mlp_tile(x_ref, sh_ref, sc_ref, gt_ref, g2_ref, w1_ref, w2_ref, gf_ref, o_ref, final):
    x = x_ref[...]
    nb, tl, _ = x.shape
    h = _rmsnorm(x, g2_ref[...]) * (1.0 + sc_ref[...]) + sh_ref[...]
    hb = h.reshape(nb * tl, D_MODEL).astype(BF16)
    acc = jnp.zeros((nb * tl, D_MODEL), F32)
    for c in range(D_FF // D_MODEL):
        cols = slice(c * D_MODEL, (c + 1) * D_MODEL)
        f = jnp.maximum(_dot(hb, w1_ref[:, cols].astype(BF16)), 0.0)
        acc = acc + _dot((f * f).astype(BF16), w2_ref[cols, :].astype(BF16))
    y = x + gt_ref[...] * acc.reshape(nb, tl, D_MODEL)
    if final:
        y = _rmsnorm(y, gf_ref[...])
    o_ref[...] = y


def _mlp_body(xp_ref, shp_ref, scp_ref, gtp_ref, xs_ref, shs_ref, scs_ref, gts_ref,
              g2_ref, w1_ref, w2_ref, gf_ref, op_ref, os_ref, *, final, n_prompt):
    i = pl.program_id(0)

    @pl.when(i < n_prompt)
    def _():
        _mlp_tile(xp_ref, shp_ref, scp_ref, gtp_ref, g2_ref, w1_ref, w2_ref, gf_ref, op_ref, final)

    @pl.when(i >= n_prompt)
    def _():
        _mlp_tile(xs_ref, shs_ref, scs_ref, gts_ref, g2_ref, w1_ref, w2_ref, gf_ref, os_ref, final)


def _mlp_call(xp, xs, ada, row0_p, row0_s, layer, w, gf, final):
    bp, lp, _ = xp.shape
    bs, ls, _ = xs.shape
    tp = lp // MLP_ROWS
    nbs = MLP_ROWS // ls
    n_prompt = bp * tp
    n_sample = bs // nbs
    assert lp % MLP_ROWS == 0 and MLP_ROWS % ls == 0 and bs % nbs == 0 and row0_s % nbs == 0
    ip = lambda i: jnp.minimum(i, n_prompt - 1)
    js = lambda i: jnp.maximum(i - n_prompt, 0)
    xp_spec = pl.BlockSpec((1, MLP_ROWS, D_MODEL), lambda i: (ip(i) // tp, ip(i) % tp, 0))
    xs_spec = pl.BlockSpec((nbs, ls, D_MODEL), lambda i: (js(i), 0, 0))
    mod_p = lambda k: pl.BlockSpec((None, 1, None, 1, D_MODEL),
                                   lambda i: (layer, row0_p + ip(i) // tp, k, 0, 0))
    mod_s = lambda k: pl.BlockSpec((None, nbs, None, 1, D_MODEL),
                                   lambda i: (layer, row0_s // nbs + js(i), k, 0, 0))
    weights = [w["g2"], w["w_mlp1"], w["w_mlp2"]]
    return pl.pallas_call(
        functools.partial(_mlp_body, final=final, n_prompt=n_prompt),
        grid=(n_prompt + n_sample,),
        in_specs=[xp_spec, mod_p(3), mod_p(4), mod_p(5), xs_spec, mod_s(3), mod_s(4), mod_s(5)]
                 + [_layer_spec(a, layer) for a in weights] + [_const_spec((1, D_MODEL))],
        out_specs=[xp_spec, xs_spec],
        out_shape=[jax.ShapeDtypeStruct(xp.shape, F32), jax.ShapeDtypeStruct(xs.shape, F32)],
        compiler_params=pltpu.CompilerParams(
            dimension_semantics=("arbitrary",), vmem_limit_bytes=VMEM_LIMIT),
        name="mlp",
    )(xp, ada, ada, ada, xs, ada, ada, ada, *weights, gf)


def _sample_proj_body(x_ref, sh_ref, sc_ref, cos_ref, sin_ref, g1_ref, w_in_ref, z_ref):
    x = x_ref[...]
    nb, tl, _ = x.shape
    h = _rmsnorm(x, g1_ref[...]) * (1.0 + sc_ref[...]) + sh_ref[...]
    hb = h.reshape(nb * tl, D_MODEL).astype(BF16)
    cos = cos_ref[...]
    sin = sin_ref[...]
    qk = _dot(hb, w_in_ref[:, OFF_Q:OFF_V])
    for hd in range(H_RET):
        cq = slice(OFF_Q + hd * DK_RET, OFF_Q + (hd + 1) * DK_RET)
        ck = slice(OFF_K + hd * DK_RET, OFF_K + (hd + 1) * DK_RET)
        z_ref[:, cq] = _rope(qk[:, cq], cos, sin)
        z_ref[:, ck] = _rope(qk[:, ck], cos, sin) * (DK_RET ** -0.5)
    for off in range(OFF_V, D_IN, D_MODEL):
        z_ref[:, off:off + D_MODEL] = _dot(hb, w_in_ref[:, off:off + D_MODEL])


def _sample_proj_call(x, ada, row0, layer, cos, sin, w):
    B, L, _ = x.shape
    nb = SAMPLE_PROJ_SEQS
    mod = lambda k: _mod_spec(layer, row0, nb, k, 1)
    weights = [w["g1"], w["w_in"]]
    return pl.pallas_call(
        _sample_proj_body,
        grid=(B // nb,),
        in_specs=[pl.BlockSpec((nb, L, D_MODEL), lambda i: (i, 0, 0)), mod(0), mod(1),
                  _const_spec((nb * L, DK_RET)), _const_spec((nb * L, DK_RET))]
                 + [_layer_spec(a, layer) for a in weights],
        out_specs=pl.BlockSpec((nb * L, D_IN), lambda i: (i, 0)),
        out_shape=jax.ShapeDtypeStruct((B * L, D_IN), F32),
        compiler_params=pltpu.CompilerParams(
            dimension_semantics=("arbitrary",), vmem_limit_bytes=VMEM_LIMIT),
        name="sample_proj",
    )(x, ada, ada, cos, sin, *weights)


def _sample_ret_body(cd_ref, qk_ref, v_ref, s_ref, mask_ref, qd_ref, kd_ref, *rest, L):
    o_ref, sn_ref = rest[-2:]
    nb = s_ref.shape[0]
    for hd in range(H_RET):
        q = qk_ref[:, OFF_Q + hd * DK_RET:OFF_Q + (hd + 1) * DK_RET]
        k = qk_ref[:, OFF_K + hd * DK_RET:OFF_K + (hd + 1) * DK_RET]
        v = v_ref[:, hd * DV_RET:(hd + 1) * DV_RET]
        vb = v.astype(BF16)
        scores = _dot_nt(q.astype(BF16), k.astype(BF16)) * mask_ref[hd]
        o_intra = _dot(scores.astype(BF16), vb)
        qs = q * qd_ref[hd]
        ks = k * kd_ref[hd]
        o_inter = []
        for b in range(nb):
            rows = slice(b * L, (b + 1) * L)
            s_old = s_ref[b, hd]
            o_inter.append(_dot(qs[rows].astype(BF16), s_old.astype(BF16)))
            sn_ref[b, hd] = (s_old * cd_ref[hd]
                             + _dot_tn(ks[rows].astype(BF16), v[rows].astype(BF16)))
        o = o_intra + jnp.concatenate(o_inter, axis=0)
        o_ref[:, hd * DV_RET:(hd + 1) * DV_RET] = _groupnorm(o)


def _sample_ret_call(z, state, new_state, layer, tables, L):
    nb = SAMPLE_RET_SEQS
    B = state.shape[1]
    R = nb * L
    mask, qd, kd, cd = tables
    sspec = pl.BlockSpec((None, nb, H_RET, DK_RET, DV_RET), lambda i: (layer, i, 0, 0, 0))
    in_specs = [pl.BlockSpec(memory_space=pltpu.SMEM),
                pl.BlockSpec((R, 2 * D_RET_QK), lambda i: (i, 0)),
                pl.BlockSpec((R, D_RET_V), lambda i: (i, OFF_V // D_RET_V)),
                sspec,
                _const_spec(mask.shape), _const_spec(qd.shape), _const_spec(kd.shape)]
    args = [cd, z, z, state, mask, qd, kd]
    aliases = {}
    if new_state is not None:
        in_specs.append(pl.BlockSpec(memory_space=pl.ANY))
        args.append(new_state)
        aliases = {len(args) - 1: 1}
    return pl.pallas_call(
        functools.partial(_sample_ret_body, L=L),
        grid=(B // nb,),
        in_specs=in_specs,
        out_specs=[pl.BlockSpec((R, D_RET_V), lambda i: (i, 0)), sspec],
        out_shape=[jax.ShapeDtypeStruct((B * L, D_RET_V), F32),
                   jax.ShapeDtypeStruct(state.shape, F32)],
        input_output_aliases=aliases,
        compiler_params=pltpu.CompilerParams(
            dimension_semantics=("arbitrary",), vmem_limit_bytes=VMEM_LIMIT),
        name="sample_ret",
    )(*args)


def _sample_post_body(x_ref, gt_ref, o_ref, g_ref, xl_ref, gl_ref, ma_ref, mb_ref,
                      conv_ref, lru_ref, w_pr_ref, w_pl_ref, w_out_ref,
                      cw_ref, cb_ref, wax_ref, ba_ref, bx_ref, lam_ref,
                      xo_ref, lruo_ref, a_s, u_s):
    x = x_ref[...]
    nb, L, _ = x.shape
    R = nb * L
    p_ret = _dot((o_ref[...] * _silu(g_ref[...])).astype(BF16), w_pr_ref[...])

    xl3 = xl_ref[...].reshape(nb, L, D_LRU)
    buf = conv_ref[...]
    row = lax.broadcasted_iota(jnp.int32, (nb, L, D_LRU), 1)
    xc3 = cw_ref[CONV_W - 1:CONV_W, :] * xl3 + cb_ref[...]
    for k in range(1, CONV_W):
        src = jnp.where(row >= L - k, buf, xl3)
        xc3 = xc3 + cw_ref[CONV_W - 1 - k:CONV_W - k, :] * pltpu.roll(src, k, axis=1)
    xc = xc3.reshape(R, D_LRU)

    ba = ba_ref[...]
    bx = bx_ref[...]
    sp = _softplus(-lam_ref[...])
    hs = []
    for n in range(N_LRU_BLOCKS):
        cols = slice(n * LRU_BLOCK, (n + 1) * LRU_BLOCK)
        a_s[n], u_s[n] = _lru_gate_block(n, xc[:, cols], wax_ref, ba, bx, sp)
        hcur = lru_ref[:, cols]
        for t in range(L):
            idx = pl.ds(t, nb, stride=L)
            hcur = a_s[n, idx, :] * hcur + u_s[n, idx, :]
            u_s[n, idx, :] = hcur
        lruo_ref[:, cols] = hcur
        hs.append(u_s[n])
    y = jnp.concatenate(hs, axis=1) * jax.nn.gelu(gl_ref[...])
    p_lru = _dot(y.astype(BF16), w_pl_ref[...])

    mix = _merge_out(ma_ref[...], mb_ref[...], p_ret, p_lru, w_out_ref)
    xo_ref[...] = x + gt_ref[...] * mix.reshape(nb, L, D_MODEL)


def _sample_post_call(x, ada, row0, layer, o, z, conv_pad, state_lru, w):
    B, L, _ = x.shape
    nb = SAMPLE_POST_SEQS
    R = nb * L
    zcol = lambda off: pl.BlockSpec((R, D_MODEL), lambda i: (i, off // D_MODEL))
    weights = [w[k] for k in ("w_proj_ret", "w_proj_lru", "w_out") + _MIXER_PARAMS]
    return pl.pallas_call(
        _sample_post_body,
        grid=(B // nb,),
        in_specs=[
            pl.BlockSpec((nb, L, D_MODEL), lambda i: (i, 0, 0)),
            _mod_spec(layer, row0, nb, 2, 1),
            pl.BlockSpec((R, D_RET_V), lambda i: (i, 0)),
            zcol(OFF_G), zcol(OFF_XL), zcol(OFF_GL), zcol(OFF_MA), zcol(OFF_MB),
            pl.BlockSpec((None, nb, SUBLANES, D_LRU), lambda i: (layer, i, 0, 0)),
            pl.BlockSpec((None, nb, D_LRU), lambda i: (layer, i, 0)),
        ] + [_layer_spec(a, layer) for a in weights],
        out_specs=[pl.BlockSpec((nb, L, D_MODEL), lambda i: (i, 0, 0)),
                   pl.BlockSpec((nb, D_LRU), lambda i: (i, 0))],
        out_shape=[jax.ShapeDtypeStruct(x.shape, F32),
                   jax.ShapeDtypeStruct((B, D_LRU), F32)],
        scratch_shapes=[pltpu.VMEM((N_LRU_BLOCKS, R, LRU_BLOCK), F32),
                        pltpu.VMEM((N_LRU_BLOCKS, R, LRU_BLOCK), F32)],
        compiler_params=pltpu.CompilerParams(
            dimension_semantics=("arbitrary",), vmem_limit_bytes=VMEM_LIMIT),
        name="sample_post",
    )(x, ada, o, z, z, z, z, z, conv_pad, state_lru, *weights)


def _rope_tables(pos):
    half = DK_RET // 2
    inv = ROPE_BASE ** (-jnp.arange(half, dtype=F32) / half)
    ang = pos.astype(F32)[:, None] * inv[None, :]
    cos = jnp.cos(ang)
    sin = jnp.sin(ang)
    return jnp.concatenate([cos, cos], axis=-1), jnp.concatenate([-sin, sin], axis=-1)


def _decay_tables(C):
    log_g = jnp.log1p(-jnp.exp2(-5.0 - jnp.arange(H_RET, dtype=F32)))
    idx = jnp.arange(C, dtype=F32)
    diff = idx[:, None] - idx[None, :]
    intra = jnp.where(diff[None] >= 0,
                      jnp.exp(jnp.maximum(diff, 0.0)[None] * log_g[:, None, None]), 0.0)
    q_decay = jnp.exp((idx + 1.0)[None, :] * log_g[:, None])
    k_decay = jnp.exp((C - 1.0 - idx)[None, :] * log_g[:, None])
    chunk_decay = jnp.exp(C * log_g)
    return intra, q_decay, k_decay, chunk_decay


def _lanes(t, reps=1):
    t = jnp.tile(t, (1, reps))
    return jnp.broadcast_to(t[:, :, None], t.shape + (DK_RET,))


def kernel(x_prompt, x_sample, c_prompt, c_sample, state_ret, state_lru, state_conv, w_ada, b_ada, norm1_g, norm2_g, w_in, w_proj_ret, w_proj_lru, w_out, conv_w, conv_b, w_rg_a, b_rg_a, w_rg_x, b_rg_x, lru_lambda, w_mlp1, w_mlp2, final_g):
    bp, lp, _ = x_prompt.shape
    bs, ls, _ = x_sample.shape
    assert lp % PROMPT_TILE == 0 and PROMPT_TILE % RET_CHUNK == 0
    assert ls == SUBLANES and ls % RET_CHUNK != 0 and ls >= CONV_W - 1

    ada = _ada_call(jnp.concatenate([c_sample, c_prompt], axis=0), w_ada, b_ada)
    row0_s, row0_p = 0, bs

    row = lambda p: p.reshape(DEPTH, 1, -1)
    w = dict(
        w_in=w_in.astype(BF16), w_proj_ret=w_proj_ret.astype(BF16),
        w_proj_lru=w_proj_lru.astype(BF16), w_out=w_out.astype(BF16),
        w_rg=jnp.concatenate([w_rg_a, w_rg_x], axis=-1).astype(BF16),
        w_mlp1=w_mlp1, w_mlp2=w_mlp2,
        conv_w=conv_w, conv_b=row(conv_b), b_rg_a=row(b_rg_a), b_rg_x=row(b_rg_x),
        lru_lambda=row(lru_lambda), g1=row(norm1_g), g2=row(norm2_g))
    gf = final_g.reshape(1, D_MODEL)

    cos_p, sin_p = _rope_tables(jnp.arange(lp, dtype=jnp.int32))
    intra, qd, kd, cd = _decay_tables(RET_CHUNK)
    tables_p = (cos_p, sin_p, intra, _lanes(qd), _lanes(kd), cd)

    cos_s, sin_s = _rope_tables(PAST_LEN + jnp.arange(ls, dtype=jnp.int32))
    cos_s = jnp.tile(cos_s, (SAMPLE_PROJ_SEQS, 1))
    sin_s = jnp.tile(sin_s, (SAMPLE_PROJ_SEQS, 1))
    intra, qd, kd, cd = _decay_tables(ls)
    eye = jnp.eye(SAMPLE_RET_SEQS, dtype=F32)
    mask = jnp.einsum("ab,hqk->haqbk", eye, intra).reshape(
        H_RET, SAMPLE_RET_SEQS * ls, SAMPLE_RET_SEQS * ls)
    tables_s = (mask, _lanes(qd, SAMPLE_RET_SEQS), _lanes(kd, SAMPLE_RET_SEQS), cd)
    conv_pad = jnp.pad(state_conv, ((0, 0), (0, 0), (SUBLANES - (CONV_W - 1), 0), (0, 0)))

    xp, xs = x_prompt, x_sample
    ret_p, lru_p, conv_p = [], [], []
    ret_s = None
    lru_s, conv_s = [], []
    for l in range(DEPTH):
        xp, sr, sl, sc = _prompt_mixer_call(xp, ada, row0_p, l, tables_p, w)
        ret_p.append(sr)
        lru_p.append(sl[:, 0])
        conv_p.append(sc[:, SUBLANES - (CONV_W - 1):])
        z = _sample_proj_call(xs, ada, row0_s, l, cos_s, sin_s, w)
        o, ret_s = _sample_ret_call(z, state_ret, ret_s, l, tables_s, ls)
        xs, sl = _sample_post_call(xs, ada, row0_s, l, o, z, conv_pad, state_lru, w)
        lru_s.append(sl)
        conv_s.append(z[:, OFF_XL:OFF_GL].reshape(bs, ls, D_LRU)[:, ls - (CONV_W - 1):])
        xp, xs = _mlp_call(xp, xs, ada, row0_p, row0_s, l, w, gf, l == DEPTH - 1)
    y_prompt, y_sample = xp, xs

    return (y_prompt, y_sample, jnp.stack(ret_p), jnp.stack(lru_p), jnp.stack(conv_p),
            ret_s, jnp.stack(lru_s), jnp.stack(conv_s))
```

```python
import functools

import jax
import jax.numpy as jnp
from jax import lax
from jax.experimental import pallas as pl
from jax.experimental.pallas import tpu as pltpu

F32 = jnp.float32
BF16 = jnp.bfloat16

D_MODEL = 1024
DEPTH = 2
PAST_LEN = 16384
H_RET = 4
DK_RET = D_MODEL // 8
DV_RET = D_MODEL // 4
D_RET_QK = H_RET * DK_RET
D_RET_V = H_RET * DV_RET
RET_CHUNK = 128
ROPE_BASE = 10000.0
D_LRU = D_MODEL
N_LRU_BLOCKS = 8
LRU_BLOCK = D_LRU // N_LRU_BLOCKS
CONV_W = 4
RG_C = 8.0
D_FF = 4 * D_MODEL
NORM_EPS = 1e-6
GN_EPS = 1e-5
D_IN = 2 * D_RET_QK + 2 * D_RET_V + 2 * D_LRU + 2 * D_MODEL

OFF_Q = 0
OFF_K = D_RET_QK
OFF_V = 2 * D_RET_QK
OFF_G = OFF_V + D_RET_V
OFF_XL = OFF_G + D_RET_V
OFF_GL = OFF_XL + D_LRU
OFF_MA = OFF_GL + D_LRU
OFF_MB = OFF_MA + D_MODEL

SUBLANES = 8
VMEM_LIMIT = 60 * 1024 * 1024

PROMPT_TILE = 512
PROJ_COLS = 256
MLP_ROWS = 512
SAMPLE_PROJ_SEQS = 32
SAMPLE_RET_SEQS = 16
SAMPLE_POST_SEQS = 32


def _dot(a, b):
    return jnp.dot(a, b, preferred_element_type=F32)


def _dot_nt(a, b):
    return lax.dot_general(a, b, (((1,), (1,)), ((), ())), preferred_element_type=F32)


def _dot_tn(a, b):
    return lax.dot_general(a, b, (((0,), (0,)), ((), ())), preferred_element_type=F32)


def _rmsnorm(x, g):
    ms = jnp.mean(x * x, axis=-1, keepdims=True)
    return (x * lax.rsqrt(ms + NORM_EPS)) * g


def _silu(x):
    return x * jax.nn.sigmoid(x)


def _rope(x, cos, sin_signed):
    return x * cos + pltpu.roll(x, DK_RET // 2, axis=1) * sin_signed


def _groupnorm(o):
    mu = jnp.mean(o, axis=-1, keepdims=True)
    d = o - mu
    var = jnp.mean(d * d, axis=-1, keepdims=True)
    return d * lax.rsqrt(var + GN_EPS)


def _softplus(y):
    return jnp.maximum(y, 0.0) + jnp.log1p(jnp.exp(-jnp.abs(y)))


def _lru_gate_block(n, xc_blk, wax_ref, ba, bx, softplus_neg_lam):
    cols = slice(n * LRU_BLOCK, (n + 1) * LRU_BLOCK)
    rg = _dot(xc_blk.astype(BF16), wax_ref[n])
    r = jax.nn.sigmoid(rg[:, :LRU_BLOCK] + ba[:, cols])
    i = jax.nn.sigmoid(rg[:, LRU_BLOCK:] + bx[:, cols])
    a = jnp.exp((-RG_C) * r * softplus_neg_lam[:, cols])
    return a, jnp.sqrt(1.0 - a * a) * (i * xc_blk)


def _merge_out(ma, mb, p_ret, p_lru, w_out_ref):
    m = jax.nn.sigmoid(ma) * p_ret + jax.nn.sigmoid(mb) * p_lru
    return _dot(m.astype(BF16), w_out_ref[...])


def _ada_body(c_ref, w_ref, b_ref, o_ref):
    s = _silu(c_ref[...]).astype(BF16)
    val = _dot(s, w_ref[...].astype(BF16)) + b_ref[...]
    o_ref[...] = val.reshape(o_ref.shape)


def _ada_call(c_all, w_ada, b_ada):
    n = c_all.shape[0]
    tn = D_MODEL
    return pl.pallas_call(
        _ada_body,
        grid=(DEPTH, 6 * D_MODEL // tn),
        in_specs=[
            pl.BlockSpec((n, D_MODEL), lambda l, j: (0, 0)),
            pl.BlockSpec((None, D_MODEL, tn), lambda l, j: (l, 0, j)),
            pl.BlockSpec((None, 1, tn), lambda l, j: (l, 0, j)),
        ],
        out_specs=pl.BlockSpec((None, n, None, 1, tn), lambda l, j: (l, 0, j, 0, 0)),
        out_shape=jax.ShapeDtypeStruct((DEPTH, n, 6, 1, D_MODEL), F32),
        compiler_params=pltpu.CompilerParams(
            dimension_semantics=("arbitrary", "arbitrary"), vmem_limit_bytes=VMEM_LIMIT),
        name="ada",
    )(c_all, w_ada, b_ada.reshape(DEPTH, 1, 6 * D_MODEL))


def _prompt_mixer_body(cd_ref, x_ref, sh_ref, sc_ref, gt_ref, cos_ref, sin_ref,
                       intra_ref, qd_ref, kd_ref, g1_ref, w_in_ref, w_pr_ref, w_pl_ref, w_out_ref,
                       cw_ref, cb_ref, wax_ref, ba_ref, bx_ref, lam_ref,
                       xo_ref, sret_ref, slru_ref, sconv_ref,
                       hb_s, z_s, xpad_s, a_s, u_s, carry_s, o_s, y_s, pl_s):
    t = pl.program_id(1)
    T = x_ref.shape[1]
    C = RET_CHUNK
    G = T // SUBLANES
    P = G + SUBLANES

    @pl.when(t == 0)
    def _():
        sret_ref[...] = jnp.zeros_like(sret_ref)
        slru_ref[...] = jnp.zeros_like(slru_ref)
        xpad_s[0:SUBLANES, :] = jnp.zeros((SUBLANES, D_LRU), F32)

    x = x_ref[0]
    h = _rmsnorm(x, g1_ref[...]) * (1.0 + sc_ref[0]) + sh_ref[0]
    hb_s[...] = h.astype(BF16)

    def proj(off):
        cols = slice(off, off + PROJ_COLS)
        z_s[:, cols] = _dot(hb_s[...], w_in_ref[:, cols])

    def proj_lru(off):
        cols = slice(off, off + PROJ_COLS)
        pl_s[:, cols] = _dot(y_s[...], w_pl_ref[:, cols])

    for off in range(OFF_XL, OFF_GL, PROJ_COLS):
        proj(off)
    xpad_s[SUBLANES:SUBLANES + T, :] = z_s[:, OFF_XL:OFF_GL]
    sconv_ref[...] = xpad_s[T:T + SUBLANES, :]

    pending = [o for seg in (OFF_Q, OFF_V, OFF_G, OFF_GL) for o in range(seg, seg + D_MODEL, PROJ_COLS)]
    ba = ba_ref[...]
    bx = bx_ref[...]
    sp = _softplus(-lam_ref[...])
    for n in range(N_LRU_BLOCKS):
        cols = slice(n * LRU_BLOCK, (n + 1) * LRU_BLOCK)
        xc = cw_ref[CONV_W - 1:CONV_W, cols] * xpad_s[SUBLANES:SUBLANES + T, cols] + cb_ref[:, cols]
        for k in range(1, CONV_W):
            xc = xc + (cw_ref[CONV_W - 1 - k:CONV_W - k, cols]
                       * xpad_s[SUBLANES - k:SUBLANES - k + T, cols])
        a, u = _lru_gate_block(n, xc, wax_ref, ba, bx, sp)
        for j in range(SUBLANES):
            a_s[n, j * P:j * P + G, :] = a[j * G:(j + 1) * G]
            u_s[n, j * P:j * P + G, :] = u[j * G:(j + 1) * G]
        proj(pending.pop(0))
        hloc = jnp.zeros((SUBLANES, LRU_BLOCK), F32)
        acum = jnp.ones((SUBLANES, LRU_BLOCK), F32)
        for i in range(G):
            idx = pl.ds(i, SUBLANES, stride=P)
            ai = a_s[n, idx, :]
            hloc = ai * hloc + u_s[n, idx, :]
            acum = acum * ai
            u_s[n, idx, :] = hloc
            a_s[n, idx, :] = acum
        carry = slru_ref[:, cols]
        for j in range(SUBLANES):
            carry_s[j:j + 1, cols] = carry
            carry = hloc[j:j + 1, :] + acum[j:j + 1, :] * carry
        slru_ref[:, cols] = carry
        proj(pending.pop(0))
    xpad_s[0:SUBLANES, :] = xpad_s[T:T + SUBLANES, :]
    for n in range(N_LRU_BLOCKS):
        cols = slice(n * LRU_BLOCK, (n + 1) * LRU_BLOCK)
        for j in range(SUBLANES):
            rows = slice(j * G, (j + 1) * G)
            prow = slice(j * P, j * P + G)
            hseq = u_s[n, prow, :] + a_s[n, prow, :] * carry_s[j:j + 1, cols]
            y_s[rows, cols] = (hseq * jax.nn.gelu(z_s[rows, OFF_GL + n * LRU_BLOCK:
                                                      OFF_GL + (n + 1) * LRU_BLOCK])).astype(BF16)

    pending = [(proj, o) for o in range(OFF_MA, D_IN, PROJ_COLS)]
    pending += [(proj_lru, o) for o in range(0, D_MODEL, PROJ_COLS)]
    n_bodies = (T // C) * H_RET
    for c in range(T // C):
        rows = slice(c * C, (c + 1) * C)
        cos = cos_ref[rows, :]
        sin = sin_ref[rows, :]
        for hd in range(H_RET):
            cq = slice(OFF_Q + hd * DK_RET, OFF_Q + (hd + 1) * DK_RET)
            ck = slice(OFF_K + hd * DK_RET, OFF_K + (hd + 1) * DK_RET)
            cv = slice(OFF_V + hd * DV_RET, OFF_V + (hd + 1) * DV_RET)
            qr = _rope(z_s[rows, cq], cos, sin)
            kr = _rope(z_s[rows, ck], cos, sin) * (DK_RET ** -0.5)
            vb = z_s[rows, cv].astype(BF16)
            s_old = sret_ref[hd]
            scores = _dot_nt(qr.astype(BF16), kr.astype(BF16)) * intra_ref[hd]
            o = _dot(jnp.concatenate([scores.astype(BF16), (qr * qd_ref[hd]).astype(BF16)], axis=1),
                     jnp.concatenate([vb, s_old.astype(BF16)], axis=0))
            sret_ref[hd] = (s_old * cd_ref[hd]
                            + _dot_tn((kr * kd_ref[hd]).astype(BF16), vb))
            co = slice(hd * DV_RET, (hd + 1) * DV_RET)
            g = z_s[rows, OFF_G + hd * DV_RET:OFF_G + (hd + 1) * DV_RET]
            o_s[rows, co] = (_groupnorm(o) * _silu(g)).astype(BF16)
            body = c * H_RET + hd
            for fn, off in pending[body * len(pending) // n_bodies:
                                   (body + 1) * len(pending) // n_bodies]:
                fn(off)

    p_ret = _dot(o_s[...], w_pr_ref[...])
    m = (jax.nn.sigmoid(z_s[:, OFF_MA:OFF_MB]) * p_ret
         + jax.nn.sigmoid(z_s[:, OFF_MB:D_IN]) * pl_s[...])
    mix = _dot(m.astype(BF16), w_out_ref[...])
    xo_ref[0] = x + gt_ref[0] * mix


def _const_spec(shape):
    nd = len(shape)
    return pl.BlockSpec(shape, lambda *_: (0,) * nd, pipeline_mode=pl.Buffered(1))


def _layer_spec(arr, layer):
    nd = arr.ndim - 1
    return pl.BlockSpec((None,) + arr.shape[1:], lambda *_: (layer,) + (0,) * nd,
                        pipeline_mode=pl.Buffered(1))


def _mod_spec(layer, row0, nb, k, grid_rank):
    assert row0 % nb == 0
    if grid_rank == 2:
        imap = lambda i, t: (layer, row0 // nb + i, k, 0, 0)
    else:
        imap = lambda i: (layer, row0 // nb + i, k, 0, 0)
    return pl.BlockSpec((None, nb, None, 1, D_MODEL), imap)


_MIXER_PARAMS = ("conv_w", "conv_b", "w_rg", "b_rg_a", "b_rg_x", "lru_lambda")


def _prompt_mixer_call(x, ada, row0, layer, tables, w):
    B, L, _ = x.shape
    T = PROMPT_TILE
    cos, sin, intra, qd, kd, cd = tables
    mod = lambda k: _mod_spec(layer, row0, 1, k, 2)
    weights = [w[k] for k in ("g1", "w_in", "w_proj_ret", "w_proj_lru", "w_out") + _MIXER_PARAMS]
    in_specs = [
        pl.BlockSpec(memory_space=pltpu.SMEM),
        pl.BlockSpec((1, T, D_MODEL), lambda b, t: (b, t, 0)),
        mod(0), mod(1), mod(2),
        pl.BlockSpec((T, DK_RET), lambda b, t: (t, 0)),
        pl.BlockSpec((T, DK_RET), lambda b, t: (t, 0)),
        _const_spec(intra.shape), _const_spec(qd.shape), _const_spec(kd.shape),
    ] + [_layer_spec(a, layer) for a in weights]
    out_specs = [
        pl.BlockSpec((1, T, D_MODEL), lambda b, t: (b, t, 0)),
        pl.BlockSpec((None, H_RET, DK_RET, DV_RET), lambda b, t: (b, 0, 0, 0)),
        pl.BlockSpec((None, 1, D_LRU), lambda b, t: (b, 0, 0)),
        pl.BlockSpec((None, SUBLANES, D_LRU), lambda b, t: (b, 0, 0)),
    ]
    out_shape = [
        jax.ShapeDtypeStruct((B, L, D_MODEL), F32),
        jax.ShapeDtypeStruct((B, H_RET, DK_RET, DV_RET), F32),
        jax.ShapeDtypeStruct((B, 1, D_LRU), F32),
        jax.ShapeDtypeStruct((B, SUBLANES, D_LRU), F32),
    ]
    scan_rows = SUBLANES * (T // SUBLANES + SUBLANES)
    scratch = [
        pltpu.VMEM((T, D_MODEL), BF16),
        pltpu.VMEM((T, D_IN), F32),
        pltpu.VMEM((T + SUBLANES, D_LRU), F32),
        pltpu.VMEM((N_LRU_BLOCKS, scan_rows, LRU_BLOCK), F32),
        pltpu.VMEM((N_LRU_BLOCKS, scan_rows, LRU_BLOCK), F32),
        pltpu.VMEM((SUBLANES, D_LRU), F32),
        pltpu.VMEM((T, D_RET_V), BF16),
        pltpu.VMEM((T, D_LRU), BF16),
        pltpu.VMEM((T, D_MODEL), F32),
    ]
    return pl.pallas_call(
        _prompt_mixer_body,
        grid=(B, L // T),
        in_specs=in_specs, out_specs=out_specs, out_shape=out_shape,
        scratch_shapes=scratch,
        compiler_params=pltpu.CompilerParams(
            dimension_semantics=("arbitrary", "arbitrary"), vmem_limit_bytes=VMEM_LIMIT),
        name="prompt_mixer",
    )(cd, x, ada, ada, ada, cos, sin, intra, qd, kd, *weights)


def _mlp_tile(x_ref, sh_ref, sc_ref, gt_ref, g2_ref, w1_ref, w2_ref, gf_ref, o_ref, final):
    x = x_ref[...]
    nb, tl, _ = x.shape
    h = _rmsnorm(x, g2_ref[...]) * (1.0 + sc_ref[...]) + sh_ref[...]
    hb = h.reshape(nb * tl, D_MODEL).astype(BF16)
    acc = jnp.zeros((nb * tl, D_MODEL), F32)
    for c in range(D_FF // D_MODEL):
        cols = slice(c * D_MODEL, (c + 1) * D_MODEL)
        f = jnp.maximum(_dot(hb, w1_ref[:, cols].astype(BF16)), 0.0)
        acc = acc + _dot((f * f).astype(BF16), w2_ref[cols, :].astype(BF16))
    y = x + gt_ref[...] * acc.reshape(nb, tl, D_MODEL)
    if final:
        y = _rmsnorm(y, gf_ref[...])
    o_ref[...] = y


def _mlp_body(xp_ref, shp_ref, scp_ref, gtp_ref, xs_ref, shs_ref, scs_ref, gts_ref,
              g2_ref, w1_ref, w2_ref, gf_ref, op_ref, os_ref, *, final, n_prompt):
    i = pl.program_id(0)

    @pl.when(i < n_prompt)
    def _():
        _mlp_tile(xp_ref, shp_ref, scp_ref, gtp_ref, g2_ref, w1_ref, w2_ref, gf_ref, op_ref, final)

    @pl.when(i >= n_prompt)
    def _():
        _mlp_tile(xs_ref, shs_ref, scs_ref, gts_ref, g2_ref, w1_ref, w2_ref, gf_ref, os_ref, final)


def _mlp_call(xp, xs, ada, row0_p, row0_s, layer, w, gf, final):
    bp, lp, _ = xp.shape
    bs, ls, _ = xs.shape
    tp = lp // MLP_ROWS
    nbs = MLP_ROWS // ls
    n_prompt = bp * tp
    n_sample = bs // nbs
    assert lp % MLP_ROWS == 0 and MLP_ROWS % ls == 0 and bs % nbs == 0 and row0_s % nbs == 0
    ip = lambda i: jnp.minimum(i, n_prompt - 1)
    js = lambda i: jnp.maximum(i - n_prompt, 0)
    xp_spec = pl.BlockSpec((1, MLP_ROWS, D_MODEL), lambda i: (ip(i) // tp, ip(i) % tp, 0))
    xs_spec = pl.BlockSpec((nbs, ls, D_MODEL), lambda i: (js(i), 0, 0))
    mod_p = lambda k: pl.BlockSpec((None, 1, None, 1, D_MODEL),
                                   lambda i: (layer, row0_p + ip(i) // tp, k, 0, 0))
    mod_s = lambda k: pl.BlockSpec((None, nbs, None, 1, D_MODEL),
                                   lambda i: (layer, row0_s // nbs + js(i), k, 0, 0))
    weights = [w["g2"], w["w_mlp1"], w["w_mlp2"]]
    return pl.pallas_call(
        functools.partial(_mlp_body, final=final, n_prompt=n_prompt),
        grid=(n_prompt + n_sample,),
        in_specs=[xp_spec, mod_p(3), mod_p(4), mod_p(5), xs_spec, mod_s(3), mod_s(4), mod_s(5)]
                 + [_layer_spec(a, layer) for a in weights] + [_const_spec((1, D_MODEL))],
        out_specs=[xp_spec, xs_spec],
        out_shape=[jax.ShapeDtypeStruct(xp.shape, F32), jax.ShapeDtypeStruct(xs.shape, F32)],
        compiler_params=pltpu.CompilerParams(
            dimension_semantics=("arbitrary",), vmem_limit_bytes=VMEM_LIMIT),
        name="mlp",
    )(xp, ada, ada, ada, xs, ada, ada, ada, *weights, gf)


def _sample_proj_body(x_ref, sh_ref, sc_ref, cos_ref, sin_ref, g1_ref, w_in_ref, z_ref):
    x = x_ref[...]
    nb, tl, _ = x.shape
    h = _rmsnorm(x, g1_ref[...]) * (1.0 + sc_ref[...]) + sh_ref[...]
    hb = h.reshape(nb * tl, D_MODEL).astype(BF16)
    cos = cos_ref[...]
    sin = sin_ref[...]
    qk = _dot(hb, w_in_ref[:, OFF_Q:OFF_V])
    for hd in range(H_RET):
        cq = slice(OFF_Q + hd * DK_RET, OFF_Q + (hd + 1) * DK_RET)
        ck = slice(OFF_K + hd * DK_RET, OFF_K + (hd + 1) * DK_RET)
        z_ref[:, cq] = _rope(qk[:, cq], cos, sin)
        z_ref[:, ck] = _rope(qk[:, ck], cos, sin) * (DK_RET ** -0.5)
    for off in range(OFF_V, D_IN, D_MODEL):
        z_ref[:, off:off + D_MODEL] = _dot(hb, w_in_ref[:, off:off + D_MODEL])


def _sample_proj_call(x, ada, row0, layer, cos, sin, w):
    B, L, _ = x.shape
    nb = SAMPLE_PROJ_SEQS
    mod = lambda k: _mod_spec(layer, row0, nb, k, 1)
    weights = [w["g1"], w["w_in"]]
    return pl.pallas_call(
        _sample_proj_body,
        grid=(B // nb,),
        in_specs=[pl.BlockSpec((nb, L, D_MODEL), lambda i: (i, 0, 0)), mod(0), mod(1),
                  _const_spec((nb * L, DK_RET)), _const_spec((nb * L, DK_RET))]
                 + [_layer_spec(a, layer) for a in weights],
        out_specs=pl.BlockSpec((nb * L, D_IN), lambda i: (i, 0)),
        out_shape=jax.ShapeDtypeStruct((B * L, D_IN), F32),
        compiler_params=pltpu.CompilerParams(
            dimension_semantics=("arbitrary",), vmem_limit_bytes=VMEM_LIMIT),
        name="sample_proj",
    )(x, ada, ada, cos, sin, *weights)


def _sample_ret_body(cd_ref, qk_ref, v_ref, s_ref, mask_ref, qd_ref, kd_ref, *rest, L):
    o_ref, sn_ref = rest[-2:]
    nb = s_ref.shape[0]
    for hd in range(H_RET):
        q = qk_ref[:, OFF_Q + hd * DK_RET:OFF_Q + (hd + 1) * DK_RET]
        k = qk_ref[:, OFF_K + hd * DK_RET:OFF_K + (hd + 1) * DK_RET]
        v = v_ref[:, hd * DV_RET:(hd + 1) * DV_RET]
        vb = v.astype(BF16)
        scores = _dot_nt(q.astype(BF16), k.astype(BF16)) * mask_ref[hd]
        o_intra = _dot(scores.astype(BF16), vb)
        qs = q * qd_ref[hd]
        ks = k * kd_ref[hd]
        o_inter = []
        for b in range(nb):
            rows = slice(b * L, (b + 1) * L)
            s_old = s_ref[b, hd]
            o_inter.append(_dot(qs[rows].astype(BF16), s_old.astype(BF16)))
            sn_ref[b, hd] = (s_old * cd_ref[hd]
                             + _dot_tn(ks[rows].astype(BF16), v[rows].astype(BF16)))
        o = o_intra + jnp.concatenate(o_inter, axis=0)
        o_ref[:, hd * DV_RET:(hd + 1) * DV_RET] = _groupnorm(o)


def _sample_ret_call(z, state, new_state, layer, tables, L):
    nb = SAMPLE_RET_SEQS
    B = state.shape[1]
    R = nb * L
    mask, qd, kd, cd = tables
    sspec = pl.BlockSpec((None, nb, H_RET, DK_RET, DV_RET), lambda i: (layer, i, 0, 0, 0))
    in_specs = [pl.BlockSpec(memory_space=pltpu.SMEM),
                pl.BlockSpec((R, 2 * D_RET_QK), lambda i: (i, 0)),
                pl.BlockSpec((R, D_RET_V), lambda i: (i, OFF_V // D_RET_V)),
                sspec,
                _const_spec(mask.shape), _const_spec(qd.shape), _const_spec(kd.shape)]
    args = [cd, z, z, state, mask, qd, kd]
    aliases = {}
    if new_state is not None:
        in_specs.append(pl.BlockSpec(memory_space=pl.ANY))
        args.append(new_state)
        aliases = {len(args) - 1: 1}
    return pl.pallas_call(
        functools.partial(_sample_ret_body, L=L),
        grid=(B // nb,),
        in_specs=in_specs,
        out_specs=[pl.BlockSpec((R, D_RET_V), lambda i: (i, 0)), sspec],
        out_shape=[jax.ShapeDtypeStruct((B * L, D_RET_V), F32),
                   jax.ShapeDtypeStruct(state.shape, F32)],
        input_output_aliases=aliases,
        compiler_params=pltpu.CompilerParams(
            dimension_semantics=("arbitrary",), vmem_limit_bytes=VMEM_LIMIT),
        name="sample_ret",
    )(*args)


def _sample_post_body(x_ref, gt_ref, o_ref, g_ref, xl_ref, gl_ref, ma_ref, mb_ref,
                      conv_ref, lru_ref, w_pr_ref, w_pl_ref, w_out_ref,
                      cw_ref, cb_ref, wax_ref, ba_ref, bx_ref, lam_ref,
                      xo_ref, lruo_ref, a_s, u_s):
    x = x_ref[...]
    nb, L, _ = x.shape
    R = nb * L
    p_ret = _dot((o_ref[...] * _silu(g_ref[...])).astype(BF16), w_pr_ref[...])

    xl3 = xl_ref[...].reshape(nb, L, D_LRU)
    buf = conv_ref[...]
    row = lax.broadcasted_iota(jnp.int32, (nb, L, D_LRU), 1)
    xc3 = cw_ref[CONV_W - 1:CONV_W, :] * xl3 + cb_ref[...]
    for k in range(1, CONV_W):
        src = jnp.where(row >= L - k, buf, xl3)
        xc3 = xc3 + cw_ref[CONV_W - 1 - k:CONV_W - k, :] * pltpu.roll(src, k, axis=1)
    xc = xc3.reshape(R, D_LRU)

    ba = ba_ref[...]
    bx = bx_ref[...]
    sp = _softplus(-lam_ref[...])
    hs = []
    for n in range(N_LRU_BLOCKS):
        cols = slice(n * LRU_BLOCK, (n + 1) * LRU_BLOCK)
        a_s[n], u_s[n] = _lru_gate_block(n, xc[:, cols], wax_ref, ba, bx, sp)
        hcur = lru_ref[:, cols]
        for t in range(L):
            idx = pl.ds(t, nb, stride=L)
            hcur = a_s[n, idx, :] * hcur + u_s[n, idx, :]
            u_s[n, idx, :] = hcur
        lruo_ref[:, cols] = hcur
        hs.append(u_s[n])
    y = jnp.concatenate(hs, axis=1) * jax.nn.gelu(gl_ref[...])
    p_lru = _dot(y.astype(BF16), w_pl_ref[...])

    mix = _merge_out(ma_ref[...], mb_ref[...], p_ret, p_lru, w_out_ref)
    xo_ref[...] = x + gt_ref[...] * mix.reshape(nb, L, D_MODEL)


def _sample_post_call(x, ada, row0, layer, o, z, conv_pad, state_lru, w):
    B, L, _ = x.shape
    nb = SAMPLE_POST_SEQS
    R = nb * L
    zcol = lambda off: pl.BlockSpec((R, D_MODEL), lambda i: (i, off // D_MODEL))
    weights = [w[k] for k in ("w_proj_ret", "w_proj_lru", "w_out") + _MIXER_PARAMS]
    return pl.pallas_call(
        _sample_post_body,
        grid=(B // nb,),
        in_specs=[
            pl.BlockSpec((nb, L, D_MODEL), lambda i: (i, 0, 0)),
            _mod_spec(layer, row0, nb, 2, 1),
            pl.BlockSpec((R, D_RET_V), lambda i: (i, 0)),
            zcol(OFF_G), zcol(OFF_XL), zcol(OFF_GL), zcol(OFF_MA), zcol(OFF_MB),
            pl.BlockSpec((None, nb, SUBLANES, D_LRU), lambda i: (layer, i, 0, 0)),
            pl.BlockSpec((None, nb, D_LRU), lambda i: (layer, i, 0)),
        ] + [_layer_spec(a, layer) for a in weights],
        out_specs=[pl.BlockSpec((nb, L, D_MODEL), lambda i: (i, 0, 0)),
                   pl.BlockSpec((nb, D_LRU), lambda i: (i, 0))],
        out_shape=[jax.ShapeDtypeStruct(x.shape, F32),
                   jax.ShapeDtypeStruct((B, D_LRU), F32)],
        scratch_shapes=[pltpu.VMEM((N_LRU_BLOCKS, R, LRU_BLOCK), F32),
                        pltpu.VMEM((N_LRU_BLOCKS, R, LRU_BLOCK), F32)],
        compiler_params=pltpu.CompilerParams(
            dimension_semantics=("arbitrary",), vmem_limit_bytes=VMEM_LIMIT),
        name="sample_post",
    )(x, ada, o, z, z, z, z, z, conv_pad, state_lru, *weights)


def _rope_tables(pos):
    half = DK_RET // 2
    inv = ROPE_BASE ** (-jnp.arange(half, dtype=F32) / half)
    ang = pos.astype(F32)[:, None] * inv[None, :]
    cos = jnp.cos(ang)
    sin = jnp.sin(ang)
    return jnp.concatenate([cos, cos], axis=-1), jnp.concatenate([-sin, sin], axis=-1)


def _decay_tables(C):
    log_g = jnp.log1p(-jnp.exp2(-5.0 - jnp.arange(H_RET, dtype=F32)))
    idx = jnp.arange(C, dtype=F32)
    diff = idx[:, None] - idx[None, :]
    intra = jnp.where(diff[None] >= 0,
                      jnp.exp(jnp.maximum(diff, 0.0)[None] * log_g[:, None, None]), 0.0)
    q_decay = jnp.exp((idx + 1.0)[None, :] * log_g[:, None])
    k_decay = jnp.exp((C - 1.0 - idx)[None, :] * log_g[:, None])
    chunk_decay = jnp.exp(C * log_g)
    return intra, q_decay, k_decay, chunk_decay


def _lanes(t, reps=1):
    t = jnp.tile(t, (1, reps))
    return jnp.broadcast_to(t[:, :, None], t.shape + (DK_RET,))


def kernel(x_prompt, x_sample, c_prompt, c_sample, state_ret, state_lru, state_conv, w_ada, b_ada, norm1_g, norm2_g, w_in, w_proj_ret, w_proj_lru, w_out, conv_w, conv_b, w_rg_a, b_rg_a, w_rg_x, b_rg_x, lru_lambda, w_mlp1, w_mlp2, final_g):
    bp, lp, _ = x_prompt.shape
    bs, ls, _ = x_sample.shape
    assert lp % PROMPT_TILE == 0 and PROMPT_TILE % RET_CHUNK == 0
    assert ls == SUBLANES and ls % RET_CHUNK != 0 and ls >= CONV_W - 1

    ada = _ada_call(jnp.concatenate([c_sample, c_prompt], axis=0), w_ada, b_ada)
    row0_s, row0_p = 0, bs

    row = lambda p: p.reshape(DEPTH, 1, -1)
    w = dict(
        w_in=w_in.astype(BF16), w_proj_ret=w_proj_ret.astype(BF16),
        w_proj_lru=w_proj_lru.astype(BF16), w_out=w_out.astype(BF16),
        w_rg=jnp.concatenate([w_rg_a, w_rg_x], axis=-1).astype(BF16),
        w_mlp1=w_mlp1, w_mlp2=w_mlp2,
        conv_w=conv_w, conv_b=row(conv_b), b_rg_a=row(b_rg_a), b_rg_x=row(b_rg_x),
        lru_lambda=row(lru_lambda), g1=row(norm1_g), g2=row(norm2_g))
    gf = final_g.reshape(1, D_MODEL)

    cos_p, sin_p = _rope_tables(jnp.arange(lp, dtype=jnp.int32))
    intra, qd, kd, cd = _decay_tables(RET_CHUNK)
    tables_p = (cos_p, sin_p, intra, _lanes(qd), _lanes(kd), cd)

    cos_s, sin_s = _rope_tables(PAST_LEN + jnp.arange(ls, dtype=jnp.int32))
    cos_s = jnp.tile(cos_s, (SAMPLE_PROJ_SEQS, 1))
    sin_s = jnp.tile(sin_s, (SAMPLE_PROJ_SEQS, 1))
    intra, qd, kd, cd = _decay_tables(ls)
    eye = jnp.eye(SAMPLE_RET_SEQS, dtype=F32)
    mask = jnp.einsum("ab,hqk->haqbk", eye, intra).reshape(
        H_RET, SAMPLE_RET_SEQS * ls, SAMPLE_RET_SEQS * ls)
    tables_s = (mask, _lanes(qd, SAMPLE_RET_SEQS), _lanes(kd, SAMPLE_RET_SEQS), cd)
    conv_pad = jnp.pad(state_conv, ((0, 0), (0, 0), (SUBLANES - (CONV_W - 1), 0), (0, 0)))

    xp, xs = x_prompt, x_sample
    ret_p, lru_p, conv_p = [], [], []
    ret_s = None
    lru_s, conv_s = [], []
    for l in range(DEPTH):
        xp, sr, sl, sc = _prompt_mixer_call(xp, ada, row0_p, l, tables_p, w)
        ret_p.append(sr)
        lru_p.append(sl[:, 0])
        conv_p.append(sc[:, SUBLANES - (CONV_W - 1):])
        z = _sample_proj_call(xs, ada, row0_s, l, cos_s, sin_s, w)
        o, ret_s = _sample_ret_call(z, state_ret, ret_s, l, tables_s, ls)
        xs, sl = _sample_post_call(xs, ada, row0_s, l, o, z, conv_pad, state_lru, w)
        lru_s.append(sl)
        conv_s.append(z[:, OFF_XL:OFF_GL].reshape(bs, ls, D_LRU)[:, ls - (CONV_W - 1):])
        xp, xs = _mlp_call(xp, xs, ada, row0_p, row0_s, l, w, gf, l == DEPTH - 1)
    y_prompt, y_sample = xp, xs

    return (y_prompt, y_sample, jnp.stack(ret_p), jnp.stack(lru_p), jnp.stack(conv_p),
            ret_s, jnp.stack(lru_s), jnp.stack(conv_s))
```

```python
import functools

import jax
import jax.numpy as jnp
from jax import lax
from jax.experimental import pallas as pl
from jax.experimental.pallas import tpu as pltpu

F32 = jnp.float32
BF16 = jnp.bfloat16

D_MODEL = 1024
DEPTH = 2
PAST_LEN = 16384
H_RET = 4
DK_RET = D_MODEL // 8
DV_RET = D_MODEL // 4
D_RET_QK = H_RET * DK_RET
D_RET_V = H_RET * DV_RET
RET_CHUNK = 128
ROPE_BASE = 10000.0
D_LRU = D_MODEL
N_LRU_BLOCKS = 8
LRU_BLOCK = D_LRU // N_LRU_BLOCKS
CONV_W = 4
RG_C = 8.0
D_FF = 4 * D_MODEL
NORM_EPS = 1e-6
GN_EPS = 1e-5
D_IN = 2 * D_RET_QK + 2 * D_RET_V + 2 * D_LRU + 2 * D_MODEL

OFF_Q = 0
OFF_K = D_RET_QK
OFF_V = 2 * D_RET_QK
OFF_G = OFF_V + D_RET_V
OFF_XL = OFF_G + D_RET_V
OFF_GL = OFF_XL + D_LRU
OFF_MA = OFF_GL + D_LRU
OFF_MB = OFF_MA + D_MODEL

SUBLANES = 8
VMEM_LIMIT = 60 * 1024 * 1024

PROMPT_TILE = 512
PROJ_COLS = 256
MLP_ROWS = 512
SAMPLE_PROJ_SEQS = 32
SAMPLE_RET_SEQS = 8
SAMPLE_POST_SEQS = 32


def _dot(a, b):
    return jnp.dot(a, b, preferred_element_type=F32)


def _dot_nt(a, b):
    return lax.dot_general(a, b, (((1,), (1,)), ((), ())), preferred_element_type=F32)


def _dot_tn(a, b):
    return lax.dot_general(a, b, (((0,), (0,)), ((), ())), preferred_element_type=F32)


def _rmsnorm(x, g):
    ms = jnp.mean(x * x, axis=-1, keepdims=True)
    return (x * lax.rsqrt(ms + NORM_EPS)) * g


def _silu(x):
    return x * jax.nn.sigmoid(x)


def _rope(x, cos, sin_signed):
    return x * cos + pltpu.roll(x, DK_RET // 2, axis=1) * sin_signed


def _groupnorm(o):
    mu = jnp.mean(o, axis=-1, keepdims=True)
    d = o - mu
    var = jnp.mean(d * d, axis=-1, keepdims=True)
    return d * lax.rsqrt(var + GN_EPS)


def _softplus(y):
    return jnp.maximum(y, 0.0) + jnp.log1p(jnp.exp(-jnp.abs(y)))


def _lru_gate_block(n, xc_blk, wax_ref, ba, bx, softplus_neg_lam):
    cols = slice(n * LRU_BLOCK, (n + 1) * LRU_BLOCK)
    rg = _dot(xc_blk.astype(BF16), wax_ref[n])
    r = jax.nn.sigmoid(rg[:, :LRU_BLOCK] + ba[:, cols])
    i = jax.nn.sigmoid(rg[:, LRU_BLOCK:] + bx[:, cols])
    a = jnp.exp((-RG_C) * r * softplus_neg_lam[:, cols])
    return a, jnp.sqrt(1.0 - a * a) * (i * xc_blk)


def _merge_out(ma, mb, p_ret, p_lru, w_out_ref):
    m = jax.nn.sigmoid(ma) * p_ret + jax.nn.sigmoid(mb) * p_lru
    return _dot(m.astype(BF16), w_out_ref[...])


def _ada_body(c_ref, w_ref, b_ref, o_ref):
    s = _silu(c_ref[...]).astype(BF16)
    val = _dot(s, w_ref[...].astype(BF16)) + b_ref[...]
    o_ref[...] = val.reshape(o_ref.shape)


def _ada_call(c_all, w_ada, b_ada):
    n = c_all.shape[0]
    tn = D_MODEL
    return pl.pallas_call(
        _ada_body,
        grid=(DEPTH, 6 * D_MODEL // tn),
        in_specs=[
            pl.BlockSpec((n, D_MODEL), lambda l, j: (0, 0)),
            pl.BlockSpec((None, D_MODEL, tn), lambda l, j: (l, 0, j)),
            pl.BlockSpec((None, 1, tn), lambda l, j: (l, 0, j)),
        ],
        out_specs=pl.BlockSpec((None, n, None, 1, tn), lambda l, j: (l, 0, j, 0, 0)),
        out_shape=jax.ShapeDtypeStruct((DEPTH, n, 6, 1, D_MODEL), F32),
        compiler_params=pltpu.CompilerParams(
            dimension_semantics=("arbitrary", "arbitrary"), vmem_limit_bytes=VMEM_LIMIT),
        name="ada",
    )(c_all, w_ada, b_ada.reshape(DEPTH, 1, 6 * D_MODEL))


def _prompt_mixer_body(cd_ref, x_ref, xn_ref, sh_ref, sc_ref, gt_ref, cos_ref, sin_ref,
                       intra_ref, qd_ref, kd_ref, g1_ref, w_in_ref, w_pr_ref, w_pl_ref, w_out_ref,
                       cw_ref, cb_ref, wax_ref, ba_ref, bx_ref, lam_ref,
                       xo_ref, sret_ref, slru_ref, sconv_ref,
                       hb_s, z_s, xpad_s, a_s, u_s, carry_s, o_s, y_s, pl_s):
    t = pl.program_id(1)
    T = x_ref.shape[1]
    C = RET_CHUNK
    G = T // SUBLANES
    P = G + SUBLANES

    @pl.when(t == 0)
    def _():
        sret_ref[...] = jnp.zeros_like(sret_ref)
        slru_ref[...] = jnp.zeros_like(slru_ref)
        xpad_s[0:SUBLANES, :] = jnp.zeros((SUBLANES, D_LRU), F32)
        hb_s[...] = _mixer_input(x_ref[0], g1_ref, sc_ref, sh_ref)

    def proj(off):
        cols = slice(off, off + PROJ_COLS)
        z_s[:, cols] = _dot(hb_s[...], w_in_ref[:, cols])

    def proj_lru(off):
        cols = slice(off, off + PROJ_COLS)
        pl_s[:, cols] = _dot(y_s[...], w_pl_ref[:, cols])

    for off in range(OFF_XL, OFF_GL, PROJ_COLS):
        proj(off)
    xpad_s[SUBLANES:SUBLANES + T, :] = z_s[:, OFF_XL:OFF_GL]
    sconv_ref[...] = xpad_s[T:T + SUBLANES, :]

    pending = [o for seg in (OFF_Q, OFF_V, OFF_G, OFF_GL) for o in range(seg, seg + D_MODEL, PROJ_COLS)]
    ba = ba_ref[...]
    bx = bx_ref[...]
    sp = _softplus(-lam_ref[...])
    for n in range(N_LRU_BLOCKS):
        cols = slice(n * LRU_BLOCK, (n + 1) * LRU_BLOCK)
        xc = cw_ref[CONV_W - 1:CONV_W, cols] * xpad_s[SUBLANES:SUBLANES + T, cols] + cb_ref[:, cols]
        for k in range(1, CONV_W):
            xc = xc + (cw_ref[CONV_W - 1 - k:CONV_W - k, cols]
                       * xpad_s[SUBLANES - k:SUBLANES - k + T, cols])
        a, u = _lru_gate_block(n, xc, wax_ref, ba, bx, sp)
        for j in range(SUBLANES):
            a_s[n, j * P:j * P + G, :] = a[j * G:(j + 1) * G]
            u_s[n, j * P:j * P + G, :] = u[j * G:(j + 1) * G]
        proj(pending.pop(0))
        hloc = jnp.zeros((SUBLANES, LRU_BLOCK), F32)
        acum = jnp.ones((SUBLANES, LRU_BLOCK), F32)
        for i in range(G):
            idx = pl.ds(i, SUBLANES, stride=P)
            ai = a_s[n, idx, :]
            hloc = ai * hloc + u_s[n, idx, :]
            acum = acum * ai
            u_s[n, idx, :] = hloc
            a_s[n, idx, :] = acum
        carry = slru_ref[:, cols]
        for j in range(SUBLANES):
            carry_s[j:j + 1, cols] = carry
            carry = hloc[j:j + 1, :] + acum[j:j + 1, :] * carry
        slru_ref[:, cols] = carry
        proj(pending.pop(0))
    xpad_s[0:SUBLANES, :] = xpad_s[T:T + SUBLANES, :]
    for n in range(N_LRU_BLOCKS):
        cols = slice(n * LRU_BLOCK, (n + 1) * LRU_BLOCK)
        for j in range(SUBLANES):
            rows = slice(j * G, (j + 1) * G)
            prow = slice(j * P, j * P + G)
            hseq = u_s[n, prow, :] + a_s[n, prow, :] * carry_s[j:j + 1, cols]
            y_s[rows, cols] = (hseq * jax.nn.gelu(z_s[rows, OFF_GL + n * LRU_BLOCK:
                                                      OFF_GL + (n + 1) * LRU_BLOCK])).astype(BF16)

    pending = [(proj, o) for o in range(OFF_MA, D_IN, PROJ_COLS)]
    pending += [(proj_lru, o) for o in range(0, D_MODEL, PROJ_COLS)]
    n_bodies = (T // C) * H_RET
    for c in range(T // C):
        rows = slice(c * C, (c + 1) * C)
        cos = cos_ref[rows, :]
        sin = sin_ref[rows, :]
        for hd in range(H_RET):
            cq = slice(OFF_Q + hd * DK_RET, OFF_Q + (hd + 1) * DK_RET)
            ck = slice(OFF_K + hd * DK_RET, OFF_K + (hd + 1) * DK_RET)
            cv = slice(OFF_V + hd * DV_RET, OFF_V + (hd + 1) * DV_RET)
            qr = _rope(z_s[rows, cq], cos, sin)
            kr = _rope(z_s[rows, ck], cos, sin) * (DK_RET ** -0.5)
            vb = z_s[rows, cv].astype(BF16)
            s_old = sret_ref[hd]
            scores = _dot_nt(qr.astype(BF16), kr.astype(BF16)) * intra_ref[hd]
            o = (_dot(scores.astype(BF16), vb)
                 + _dot((qr * qd_ref[hd]).astype(BF16), s_old.astype(BF16)))
            sret_ref[hd] = (s_old * cd_ref[hd]
                            + _dot_tn((kr * kd_ref[hd]).astype(BF16), vb))
            co = slice(hd * DV_RET, (hd + 1) * DV_RET)
            g = z_s[rows, OFF_G + hd * DV_RET:OFF_G + (hd + 1) * DV_RET]
            o_s[rows, co] = (_groupnorm(o) * _silu(g)).astype(BF16)
            body = c * H_RET + hd
            for fn, off in pending[body * len(pending) // n_bodies:
                                   (body + 1) * len(pending) // n_bodies]:
                fn(off)

    hb_s[...] = _mixer_input(xn_ref[0], g1_ref, sc_ref, sh_ref)
    p_ret = _dot(o_s[...], w_pr_ref[...])
    m = (jax.nn.sigmoid(z_s[:, OFF_MA:OFF_MB]) * p_ret
         + jax.nn.sigmoid(z_s[:, OFF_MB:D_IN]) * pl_s[...])
    mix = _dot(m.astype(BF16), w_out_ref[...])
    xo_ref[0] = x_ref[0] + gt_ref[0] * mix


def _mixer_input(x, g1_ref, sc_ref, sh_ref):
    return (_rmsnorm(x, g1_ref[...]) * (1.0 + sc_ref[0]) + sh_ref[0]).astype(BF16)


def _const_spec(shape):
    nd = len(shape)
    return pl.BlockSpec(shape, lambda *_: (0,) * nd, pipeline_mode=pl.Buffered(1))


def _layer_spec(arr, layer):
    nd = arr.ndim - 1
    return pl.BlockSpec((None,) + arr.shape[1:], lambda *_: (layer,) + (0,) * nd,
                        pipeline_mode=pl.Buffered(1))


def _mod_spec(layer, row0, nb, k, grid_rank):
    assert row0 % nb == 0
    if grid_rank == 2:
        imap = lambda i, t: (layer, row0 // nb + i, k, 0, 0)
    else:
        imap = lambda i: (layer, row0 // nb + i, k, 0, 0)
    return pl.BlockSpec((None, nb, None, 1, D_MODEL), imap)


_MIXER_PARAMS = ("conv_w", "conv_b", "w_rg", "b_rg_a", "b_rg_x", "lru_lambda")


def _prompt_mixer_call(x, ada, row0, layer, tables, w):
    B, L, _ = x.shape
    T = PROMPT_TILE
    cos, sin, intra, qd, kd, cd = tables
    mod = lambda k: _mod_spec(layer, row0, 1, k, 2)
    weights = [w[k] for k in ("g1", "w_in", "w_proj_ret", "w_proj_lru", "w_out") + _MIXER_PARAMS]
    in_specs = [
        pl.BlockSpec(memory_space=pltpu.SMEM),
        pl.BlockSpec((1, T, D_MODEL), lambda b, t: (b, t, 0)),
        pl.BlockSpec((1, T, D_MODEL), lambda b, t: (b, jnp.minimum(t + 1, L // T - 1), 0)),
        mod(0), mod(1), mod(2),
        pl.BlockSpec((T, DK_RET), lambda b, t: (t, 0)),
        pl.BlockSpec((T, DK_RET), lambda b, t: (t, 0)),
        _const_spec(intra.shape), _const_spec(qd.shape), _const_spec(kd.shape),
    ] + [_layer_spec(a, layer) for a in weights]
    out_specs = [
        pl.BlockSpec((1, T, D_MODEL), lambda b, t: (b, t, 0)),
        pl.BlockSpec((None, H_RET, DK_RET, DV_RET), lambda b, t: (b, 0, 0, 0)),
        pl.BlockSpec((None, 1, D_LRU), lambda b, t: (b, 0, 0)),
        pl.BlockSpec((None, SUBLANES, D_LRU), lambda b, t: (b, 0, 0)),
    ]
    out_shape = [
        jax.ShapeDtypeStruct((B, L, D_MODEL), F32),
        jax.ShapeDtypeStruct((B, H_RET, DK_RET, DV_RET), F32),
        jax.ShapeDtypeStruct((B, 1, D_LRU), F32),
        jax.ShapeDtypeStruct((B, SUBLANES, D_LRU), F32),
    ]
    scan_rows = SUBLANES * (T // SUBLANES + SUBLANES)
    scratch = [
        pltpu.VMEM((T, D_MODEL), BF16),
        pltpu.VMEM((T, D_IN), F32),
        pltpu.VMEM((T + SUBLANES, D_LRU), F32),
        pltpu.VMEM((N_LRU_BLOCKS, scan_rows, LRU_BLOCK), F32),
        pltpu.VMEM((N_LRU_BLOCKS, scan_rows, LRU_BLOCK), F32),
        pltpu.VMEM((SUBLANES, D_LRU), F32),
        pltpu.VMEM((T, D_RET_V), BF16),
        pltpu.VMEM((T, D_LRU), BF16),
        pltpu.VMEM((T, D_MODEL), F32),
    ]
    return pl.pallas_call(
        _prompt_mixer_body,
        grid=(B, L // T),
        in_specs=in_specs, out_specs=out_specs, out_shape=out_shape,
        scratch_shapes=scratch,
        compiler_params=pltpu.CompilerParams(
            dimension_semantics=("arbitrary", "arbitrary"), vmem_limit_bytes=VMEM_LIMIT),
        name="prompt_mixer",
    )(cd, x, x, ada, ada, ada, cos, sin, intra, qd, kd, *weights)


def _mlp_tile(x_ref, sh_ref, sc_ref, gt_ref, g2_ref, w1_ref, w2_ref, gf_ref, o_ref, final):
    x = x_ref[...]
    nb, tl, _ = x.shape
    h = _rmsnorm(x, g2_ref[...]) * (1.0 + sc_ref[...]) + sh_ref[...]
    hb = h.reshape(nb * tl, D_MODEL).astype(BF16)
    acc = jnp.zeros((nb * tl, D_MODEL), F32)
    for c in range(D_FF // D_MODEL):
        cols = slice(c * D_MODEL, (c + 1) * D_MODEL)
        f = jnp.maximum(_dot(hb, w1_ref[:, cols].astype(BF16)), 0.0)
        acc = acc + _dot((f * f).astype(BF16), w2_ref[cols, :].astype(BF16))
    y = x + gt_ref[...] * acc.reshape(nb, tl, D_MODEL)
    if final:
        y = _rmsnorm(y, gf_ref[...])
    o_ref[...] = y


def _mlp_body(xp_ref, shp_ref, scp_ref, gtp_ref, xs_ref, shs_ref, scs_ref, gts_ref,
              g2_ref, w1_ref, w2_ref, gf_ref, op_ref, os_ref, *, final, n_prompt):
    i = pl.program_id(0)

    @pl.when(i < n_prompt)
    def _():
        _mlp_tile(xp_ref, shp_ref, scp_ref, gtp_ref, g2_ref, w1_ref, w2_ref, gf_ref, op_ref, final)

    @pl.when(i >= n_prompt)
    def _():
        _mlp_tile(xs_ref, shs_ref, scs_ref, gts_ref, g2_ref, w1_ref, w2_ref, gf_ref, os_ref, final)


def _mlp_call(xp, xs, ada, row0_p, row0_s, layer, w, gf, final):
    bp, lp, _ = xp.shape
    bs, ls, _ = xs.shape
    tp = lp // MLP_ROWS
    nbs = MLP_ROWS // ls
    n_prompt = bp * tp
    n_sample = bs // nbs
    assert lp % MLP_ROWS == 0 and MLP_ROWS % ls == 0 and bs % nbs == 0 and row0_s % nbs == 0
    ip = lambda i: jnp.minimum(i, n_prompt - 1)
    js = lambda i: jnp.maximum(i - n_prompt, 0)
    xp_spec = pl.BlockSpec((1, MLP_ROWS, D_MODEL), lambda i: (ip(i) // tp, ip(i) % tp, 0))
    xs_spec = pl.BlockSpec((nbs, ls, D_MODEL), lambda i: (js(i), 0, 0))
    mod_p = lambda k: pl.BlockSpec((None, 1, None, 1, D_MODEL),
                                   lambda i: (layer, row0_p + ip(i) // tp, k, 0, 0))
    mod_s = lambda k: pl.BlockSpec((None, nbs, None, 1, D_MODEL),
                                   lambda i: (layer, row0_s // nbs + js(i), k, 0, 0))
    weights = [w["g2"], w["w_mlp1"], w["w_mlp2"]]
    return pl.pallas_call(
        functools.partial(_mlp_body, final=final, n_prompt=n_prompt),
        grid=(n_prompt + n_sample,),
        in_specs=[xp_spec, mod_p(3), mod_p(4), mod_p(5), xs_spec, mod_s(3), mod_s(4), mod_s(5)]
                 + [_layer_spec(a, layer) for a in weights] + [_const_spec((1, D_MODEL))],
        out_specs=[xp_spec, xs_spec],
        out_shape=[jax.ShapeDtypeStruct(xp.shape, F32), jax.ShapeDtypeStruct(xs.shape, F32)],
        compiler_params=pltpu.CompilerParams(
            dimension_semantics=("arbitrary",), vmem_limit_bytes=VMEM_LIMIT),
        name="mlp",
    )(xp, ada, ada, ada, xs, ada, ada, ada, *weights, gf)


def _sample_proj_body(x_ref, sh_ref, sc_ref, cos_ref, sin_ref, g1_ref, w_in_ref, z_ref):
    x = x_ref[...]
    nb, tl, _ = x.shape
    h = _rmsnorm(x, g1_ref[...]) * (1.0 + sc_ref[...]) + sh_ref[...]
    hb = h.reshape(nb * tl, D_MODEL).astype(BF16)
    cos = cos_ref[...]
    sin = sin_ref[...]
    qk = _dot(hb, w_in_ref[:, OFF_Q:OFF_V])
    for hd in range(H_RET):
        cq = slice(OFF_Q + hd * DK_RET, OFF_Q + (hd + 1) * DK_RET)
        ck = slice(OFF_K + hd * DK_RET, OFF_K + (hd + 1) * DK_RET)
        z_ref[:, cq] = _rope(qk[:, cq], cos, sin)
        z_ref[:, ck] = _rope(qk[:, ck], cos, sin) * (DK_RET ** -0.5)
    for off in range(OFF_V, D_IN, D_MODEL):
        z_ref[:, off:off + D_MODEL] = _dot(hb, w_in_ref[:, off:off + D_MODEL])


def _sample_proj_call(x, ada, row0, layer, cos, sin, w):
    B, L, _ = x.shape
    nb = SAMPLE_PROJ_SEQS
    mod = lambda k: _mod_spec(layer, row0, nb, k, 1)
    weights = [w["g1"], w["w_in"]]
    return pl.pallas_call(
        _sample_proj_body,
        grid=(B // nb,),
        in_specs=[pl.BlockSpec((nb, L, D_MODEL), lambda i: (i, 0, 0)), mod(0), mod(1),
                  _const_spec((nb * L, DK_RET)), _const_spec((nb * L, DK_RET))]
                 + [_layer_spec(a, layer) for a in weights],
        out_specs=pl.BlockSpec((nb * L, D_IN), lambda i: (i, 0)),
        out_shape=jax.ShapeDtypeStruct((B * L, D_IN), F32),
        compiler_params=pltpu.CompilerParams(
            dimension_semantics=("arbitrary",), vmem_limit_bytes=VMEM_LIMIT),
        name="sample_proj",
    )(x, ada, ada, cos, sin, *weights)


def _sample_ret_body(cd_ref, qk_ref, v_ref, s_ref, mask_ref, qd_ref, kd_ref, *rest, L):
    o_ref, sn_ref = rest[-2:]
    nb = s_ref.shape[0]
    for hd in range(H_RET):
        q = qk_ref[:, OFF_Q + hd * DK_RET:OFF_Q + (hd + 1) * DK_RET]
        k = qk_ref[:, OFF_K + hd * DK_RET:OFF_K + (hd + 1) * DK_RET]
        v = v_ref[:, hd * DV_RET:(hd + 1) * DV_RET]
        vb = v.astype(BF16)
        scores = _dot_nt(q.astype(BF16), k.astype(BF16)) * mask_ref[hd]
        o_intra = _dot(scores.astype(BF16), vb)
        qs = q * qd_ref[hd]
        ks = k * kd_ref[hd]
        o_inter = []
        for b in range(nb):
            rows = slice(b * L, (b + 1) * L)
            s_old = s_ref[b, hd]
            o_inter.append(_dot(qs[rows].astype(BF16), s_old.astype(BF16)))
            sn_ref[b, hd] = (s_old * cd_ref[hd]
                             + _dot_tn(ks[rows].astype(BF16), v[rows].astype(BF16)))
        o = o_intra + jnp.concatenate(o_inter, axis=0)
        o_ref[:, hd * DV_RET:(hd + 1) * DV_RET] = _groupnorm(o)


def _sample_ret_call(z, state, new_state, layer, tables, L):
    nb = SAMPLE_RET_SEQS
    B = state.shape[1]
    R = nb * L
    mask, qd, kd, cd = tables
    sspec = pl.BlockSpec((None, nb, H_RET, DK_RET, DV_RET), lambda i: (layer, i, 0, 0, 0))
    in_specs = [pl.BlockSpec(memory_space=pltpu.SMEM),
                pl.BlockSpec((R, 2 * D_RET_QK), lambda i: (i, 0)),
                pl.BlockSpec((R, D_RET_V), lambda i: (i, OFF_V // D_RET_V)),
                sspec,
                _const_spec(mask.shape), _const_spec(qd.shape), _const_spec(kd.shape)]
    args = [cd, z, z, state, mask, qd, kd]
    aliases = {}
    if new_state is not None:
        in_specs.append(pl.BlockSpec(memory_space=pl.ANY))
        args.append(new_state)
        aliases = {len(args) - 1: 1}
    return pl.pallas_call(
        functools.partial(_sample_ret_body, L=L),
        grid=(B // nb,),
        in_specs=in_specs,
        out_specs=[pl.BlockSpec((R, D_RET_V), lambda i: (i, 0)), sspec],
        out_shape=[jax.ShapeDtypeStruct((B * L, D_RET_V), F32),
                   jax.ShapeDtypeStruct(state.shape, F32)],
        input_output_aliases=aliases,
        compiler_params=pltpu.CompilerParams(
            dimension_semantics=("arbitrary",), vmem_limit_bytes=VMEM_LIMIT),
        name="sample_ret",
    )(*args)


def _sample_post_body(x_ref, gt_ref, o_ref, g_ref, xl_ref, gl_ref, ma_ref, mb_ref,
                      conv_ref, lru_ref, w_pr_ref, w_pl_ref, w_out_ref,
                      cw_ref, cb_ref, wax_ref, ba_ref, bx_ref, lam_ref,
                      xo_ref, lruo_ref, a_s, u_s):
    x = x_ref[...]
    nb, L, _ = x.shape
    R = nb * L
    p_ret = _dot((o_ref[...] * _silu(g_ref[...])).astype(BF16), w_pr_ref[...])

    xl3 = xl_ref[...].reshape(nb, L, D_LRU)
    buf = conv_ref[...]
    row = lax.broadcasted_iota(jnp.int32, (nb, L, D_LRU), 1)
    xc3 = cw_ref[CONV_W - 1:CONV_W, :] * xl3 + cb_ref[...]
    for k in range(1, CONV_W):
        src = jnp.where(row >= L - k, buf, xl3)
        xc3 = xc3 + cw_ref[CONV_W - 1 - k:CONV_W - k, :] * pltpu.roll(src, k, axis=1)
    xc = xc3.reshape(R, D_LRU)

    ba = ba_ref[...]
    bx = bx_ref[...]
    sp = _softplus(-lam_ref[...])
    hs = []
    for n in range(N_LRU_BLOCKS):
        cols = slice(n * LRU_BLOCK, (n + 1) * LRU_BLOCK)
        a_s[n], u_s[n] = _lru_gate_block(n, xc[:, cols], wax_ref, ba, bx, sp)
        hcur = lru_ref[:, cols]
        for t in range(L):
            idx = pl.ds(t, nb, stride=L)
            hcur = a_s[n, idx, :] * hcur + u_s[n, idx, :]
            u_s[n, idx, :] = hcur
        lruo_ref[:, cols] = hcur
        hs.append(u_s[n])
    y = jnp.concatenate(hs, axis=1) * jax.nn.gelu(gl_ref[...])
    p_lru = _dot(y.astype(BF16), w_pl_ref[...])

    mix = _merge_out(ma_ref[...], mb_ref[...], p_ret, p_lru, w_out_ref)
    xo_ref[...] = x + gt_ref[...] * mix.reshape(nb, L, D_MODEL)


def _sample_post_call(x, ada, row0, layer, o, z, conv_pad, state_lru, w):
    B, L, _ = x.shape
    nb = SAMPLE_POST_SEQS
    R = nb * L
    zcol = lambda off: pl.BlockSpec((R, D_MODEL), lambda i: (i, off // D_MODEL))
    weights = [w[k] for k in ("w_proj_ret", "w_proj_lru", "w_out") + _MIXER_PARAMS]
    return pl.pallas_call(
        _sample_post_body,
        grid=(B // nb,),
        in_specs=[
            pl.BlockSpec((nb, L, D_MODEL), lambda i: (i, 0, 0)),
            _mod_spec(layer, row0, nb, 2, 1),
            pl.BlockSpec((R, D_RET_V), lambda i: (i, 0)),
            zcol(OFF_G), zcol(OFF_XL), zcol(OFF_GL), zcol(OFF_MA), zcol(OFF_MB),
            pl.BlockSpec((None, nb, SUBLANES, D_LRU), lambda i: (layer, i, 0, 0)),
            pl.BlockSpec((None, nb, D_LRU), lambda i: (layer, i, 0)),
        ] + [_layer_spec(a, layer) for a in weights],
        out_specs=[pl.BlockSpec((nb, L, D_MODEL), lambda i: (i, 0, 0)),
                   pl.BlockSpec((nb, D_LRU), lambda i: (i, 0))],
        out_shape=[jax.ShapeDtypeStruct(x.shape, F32),
                   jax.ShapeDtypeStruct((B, D_LRU), F32)],
        scratch_shapes=[pltpu.VMEM((N_LRU_BLOCKS, R, LRU_BLOCK), F32),
                        pltpu.VMEM((N_LRU_BLOCKS, R, LRU_BLOCK), F32)],
        compiler_params=pltpu.CompilerParams(
            dimension_semantics=("arbitrary",), vmem_limit_bytes=VMEM_LIMIT),
        name="sample_post",
    )(x, ada, o, z, z, z, z, z, conv_pad, state_lru, *weights)


def _rope_tables(pos):
    half = DK_RET // 2
    inv = ROPE_BASE ** (-jnp.arange(half, dtype=F32) / half)
    ang = pos.astype(F32)[:, None] * inv[None, :]
    cos = jnp.cos(ang)
    sin = jnp.sin(ang)
    return jnp.concatenate([cos, cos], axis=-1), jnp.concatenate([-sin, sin], axis=-1)


def _decay_tables(C):
    log_g = jnp.log1p(-jnp.exp2(-5.0 - jnp.arange(H_RET, dtype=F32)))
    idx = jnp.arange(C, dtype=F32)
    diff = idx[:, None] - idx[None, :]
    intra = jnp.where(diff[None] >= 0,
                      jnp.exp(jnp.maximum(diff, 0.0)[None] * log_g[:, None, None]), 0.0)
    q_decay = jnp.exp((idx + 1.0)[None, :] * log_g[:, None])
    k_decay = jnp.exp((C - 1.0 - idx)[None, :] * log_g[:, None])
    chunk_decay = jnp.exp(C * log_g)
    return intra, q_decay, k_decay, chunk_decay


def _lanes(t, reps=1):
    t = jnp.tile(t, (1, reps))
    return jnp.broadcast_to(t[:, :, None], t.shape + (DK_RET,))


def kernel(x_prompt, x_sample, c_prompt, c_sample, state_ret, state_lru, state_conv, w_ada, b_ada, norm1_g, norm2_g, w_in, w_proj_ret, w_proj_lru, w_out, conv_w, conv_b, w_rg_a, b_rg_a, w_rg_x, b_rg_x, lru_lambda, w_mlp1, w_mlp2, final_g):
    bp, lp, _ = x_prompt.shape
    bs, ls, _ = x_sample.shape
    assert lp % PROMPT_TILE == 0 and PROMPT_TILE % RET_CHUNK == 0
    assert ls == SUBLANES and ls % RET_CHUNK != 0 and ls >= CONV_W - 1

    ada = _ada_call(jnp.concatenate([c_sample, c_prompt], axis=0), w_ada, b_ada)
    row0_s, row0_p = 0, bs

    row = lambda p: p.reshape(DEPTH, 1, -1)
    w = dict(
        w_in=w_in.astype(BF16), w_proj_ret=w_proj_ret.astype(BF16),
        w_proj_lru=w_proj_lru.astype(BF16), w_out=w_out.astype(BF16),
        w_rg=jnp.concatenate([w_rg_a, w_rg_x], axis=-1).astype(BF16),
        w_mlp1=w_mlp1, w_mlp2=w_mlp2,
        conv_w=conv_w, conv_b=row(conv_b), b_rg_a=row(b_rg_a), b_rg_x=row(b_rg_x),
        lru_lambda=row(lru_lambda), g1=row(norm1_g), g2=row(norm2_g))
    gf = final_g.reshape(1, D_MODEL)

    cos_p, sin_p = _rope_tables(jnp.arange(lp, dtype=jnp.int32))
    intra, qd, kd, cd = _decay_tables(RET_CHUNK)
    tables_p = (cos_p, sin_p, intra, _lanes(qd), _lanes(kd), cd)

    cos_s, sin_s = _rope_tables(PAST_LEN + jnp.arange(ls, dtype=jnp.int32))
    cos_s = jnp.tile(cos_s, (SAMPLE_PROJ_SEQS, 1))
    sin_s = jnp.tile(sin_s, (SAMPLE_PROJ_SEQS, 1))
    intra, qd, kd, cd = _decay_tables(ls)
    eye = jnp.eye(SAMPLE_RET_SEQS, dtype=F32)
    mask = jnp.einsum("ab,hqk->haqbk", eye, intra).reshape(
        H_RET, SAMPLE_RET_SEQS * ls, SAMPLE_RET_SEQS * ls)
    tables_s = (mask, _lanes(qd, SAMPLE_RET_SEQS), _lanes(kd, SAMPLE_RET_SEQS), cd)
    conv_pad = jnp.pad(state_conv, ((0, 0), (0, 0), (SUBLANES - (CONV_W - 1), 0), (0, 0)))

    xp, xs = x_prompt, x_sample
    ret_p, lru_p, conv_p = [], [], []
    ret_s = None
    lru_s, conv_s = [], []
    for l in range(DEPTH):
        xp, sr, sl, sc = _prompt_mixer_call(xp, ada, row0_p, l, tables_p, w)
        ret_p.append(sr)
        lru_p.append(sl[:, 0])
        conv_p.append(sc[:, SUBLANES - (CONV_W - 1):])
        z = _sample_proj_call(xs, ada, row0_s, l, cos_s, sin_s, w)
        o, ret_s = _sample_ret_call(z, state_ret, ret_s, l, tables_s, ls)
        xs, sl = _sample_post_call(xs, ada, row0_s, l, o, z, conv_pad, state_lru, w)
        lru_s.append(sl)
        conv_s.append(z[:, OFF_XL:OFF_GL].reshape(bs, ls, D_LRU)[:, ls - (CONV_W - 1):])
        xp, xs = _mlp_call(xp, xs, ada, row0_p, row0_s, l, w, gf, l == DEPTH - 1)
    y_prompt, y_sample = xp, xs

    return (y_prompt, y_sample, jnp.stack(ret_p), jnp.stack(lru_p), jnp.stack(conv_p),
            ret_s, jnp.stack(lru_s), jnp.stack(conv_s))
```
